```python
import math
import jax, jax.numpy as jnp
from jax import lax
import numpy as np

D_MODEL = 1024
BATCH = 1
SEQ = 16384
DEPTH = 4

N_MIXERS = 3
RMS_EPS = 1e-6
CONV_W = 4
RG_WIDTH = 1280
RG_BLOCKS = 10
RG_BLOCK_DIM = RG_WIDTH // RG_BLOCKS
RG_C = 8.0
ML_INNER = 2 * D_MODEL
ML_HEADS = 4
ML_HEAD_DIM = ML_INNER // ML_HEADS
ML_QKV_BLOCK = 4
ML_QKV_BLOCKS = ML_INNER // ML_QKV_BLOCK
ML_CHUNK = 128
S5_GROUP = 16
S5_GROUPS = D_MODEL // S5_GROUP
S5_STATE = 64
D_FF = 2816
N_EXPERTS = 8
TOP_K = 2
N_RGLRU = (DEPTH + 2) // 3
N_MLSTM = (DEPTH + 1) // 3
N_S5 = DEPTH // 3
N_DENSE = (DEPTH + 1) // 2
N_MOE = DEPTH // 2

kernel_name = 'hybrid_rglru_mlstm_s5_moe_trunk'


def rms_norm(x, g):
    xf = x.astype(jnp.float32)
    y = xf * lax.rsqrt(jnp.mean(xf * xf, axis=-1, keepdims=True) + RMS_EPS)
    return (y * g.astype(jnp.float32)).astype(x.dtype)


def causal_conv(x, w, b):
    s = x.shape[1]
    xp = jnp.pad(x, ((0, 0), (CONV_W - 1, 0), (0, 0)))
    out = b + w[0] * xp[:, 0:s]
    for k in range(1, CONV_W):
        out = out + w[k] * xp[:, k:k + s]
    return out


def _lin_combine(c1, c2):
    a1, b1 = c1
    a2, b2 = c2
    return a1 * a2, a2 * b1 + b2


def linear_scan(a, b):
    _, h = lax.associative_scan(_lin_combine, (a, b), axis=1)
    return h


def swiglu(h, w1, w3, w2):
    return (jax.nn.silu(h @ w1) * (h @ w3)) @ w2


def rglru_mixer(h, w_in, conv_w, conv_b, w_r, b_r, w_i, b_i, lam, w_out):
    bsz, s, _ = h.shape
    gx, xr = jnp.split(h @ w_in, 2, axis=-1)
    xr = causal_conv(xr, conv_w, conv_b)
    xb = xr.reshape(bsz, s, RG_BLOCKS, RG_BLOCK_DIM)
    r = jax.nn.sigmoid(jnp.einsum('bsni,nij->bsnj', xb, w_r).reshape(bsz, s, RG_WIDTH) + b_r)
    ig = jax.nn.sigmoid(jnp.einsum('bsni,nij->bsnj', xb, w_i).reshape(bsz, s, RG_WIDTH) + b_i)
    log_a = (-RG_C * r * jax.nn.softplus(-lam)).astype(jnp.float32)
    a = jnp.exp(log_a)
    b = jnp.sqrt(-jnp.expm1(2.0 * log_a)) * (ig * xr).astype(jnp.float32)
    hs = linear_scan(a, b).astype(h.dtype)
    return (jax.nn.gelu(gx) * hs) @ w_out


def mlstm_chunkwise(q, k, v, ig, lf):
    q, k, v, ig, lf = (t.astype(jnp.float32) for t in (q, k, v, ig, lf))
    bsz, s = q.shape[:2]
    nc = s // ML_CHUNK

    def to_chunks(t):
        return t.reshape(bsz, nc, ML_CHUNK, ML_HEADS, ML_HEAD_DIM).transpose(1, 0, 3, 2, 4)

    def gate_chunks(t):
        return t.reshape(bsz, nc, ML_CHUNK, ML_HEADS).transpose(1, 0, 3, 2)

    mask = jnp.tril(jnp.ones((ML_CHUNK, ML_CHUNK), dtype=bool))

    def step(carry, inp):
        c_st, n_st, m_st = carry
        qc, kc, vc, ic, fc = inp
        bcum = jnp.cumsum(fc, axis=-1)
        dm = bcum[..., :, None] - bcum[..., None, :] + ic[..., None, :]
        dm = jnp.where(mask, dm, -jnp.inf)
        inter = bcum + m_st[..., None]
        m_t = jnp.maximum(inter, jnp.max(dm, axis=-1))
        dexp = jnp.exp(dm - m_t[..., None])
        sc = jnp.exp(inter - m_t)
        sco = jnp.einsum('bhtd,bhsd->bhts', qc, kc) * dexp
        num = jnp.einsum('bhts,bhse->bhte', sco, vc) + sc[..., None] * jnp.einsum('bhtd,bhde->bhte', qc, c_st)
        den = jnp.sum(sco, axis=-1) + sc * jnp.einsum('bhtd,bhd->bht', qc, n_st)
        hc = num / jnp.maximum(jnp.abs(den), jnp.exp(-m_t))[..., None]
        m_new = m_t[..., -1]
        w = jnp.exp(bcum[..., -1:] - bcum + ic - m_new[..., None])
        decay = jnp.exp(bcum[..., -1] + m_st - m_new)
        c_new = decay[..., None, None] * c_st + jnp.einsum('bhs,bhsd,bhse->bhde', w, kc, vc)
        n_new = decay[..., None] * n_st + jnp.einsum('bhs,bhsd->bhd', w, kc)
        return (c_new, n_new, m_new), hc

    init = (jnp.zeros((bsz, ML_HEADS, ML_HEAD_DIM, ML_HEAD_DIM), jnp.float32),
            jnp.zeros((bsz, ML_HEADS, ML_HEAD_DIM), jnp.float32),
            jnp.zeros((bsz, ML_HEADS), jnp.float32))
    _, hs = lax.scan(step, init, (to_chunks(q), to_chunks(k), to_chunks(v), gate_chunks(ig), gate_chunks(lf)))
    return hs.transpose(1, 0, 3, 2, 4).reshape(bsz, s, ML_HEADS, ML_HEAD_DIM)


def mlstm_mixer(h, w_in, conv_w, conv_b, w_q, w_k, w_v, w_gate, b_gate, norm_g, skip, w_out):
    bsz, s, _ = h.shape
    xm, z = jnp.split(h @ w_in, 2, axis=-1)
    xc = jax.nn.silu(causal_conv(xm, conv_w, conv_b))

    def headwise(t, w):
        tb = t.reshape(bsz, s, ML_QKV_BLOCKS, ML_QKV_BLOCK)
        return jnp.einsum('bsni,nij->bsnj', tb, w).reshape(bsz, s, ML_INNER)

    q = headwise(xc, w_q)
    k = headwise(xc, w_k)
    v = headwise(xm, w_v)
    gates = (jnp.concatenate([q, k, v], axis=-1) @ w_gate + b_gate).astype(jnp.float32)
    ig, fg = jnp.split(gates, 2, axis=-1)
    lf = jax.nn.log_sigmoid(fg)

    def heads(t):
        return t.reshape(bsz, s, ML_HEADS, ML_HEAD_DIM)

    hc = mlstm_chunkwise(heads(q), heads(k) * (ML_HEAD_DIM ** -0.5), heads(v), ig, lf)
    hn = hc * lax.rsqrt(jnp.mean(hc * hc, axis=-1, keepdims=True) + RMS_EPS)
    hn = (hn.reshape(bsz, s, ML_INNER) * norm_g.astype(jnp.float32)).astype(h.dtype)
    y = jax.nn.sigmoid(z) * (hn + skip * xc)
    return y @ w_out


def s5_mixer(h, a_re, a_im, log_step, b_re, b_im, c_re, c_im, d_skip, w_glu, b_glu):
    bsz, s, _ = h.shape
    f32 = jnp.float32
    u = h.astype(f32).reshape(bsz, s, S5_GROUPS, S5_GROUP)
    lam = lax.complex(a_re.astype(f32), a_im.astype(f32))
    step = jnp.exp(log_step.astype(f32))[:, None]
    lam_bar = jnp.exp(lam * step)
    b_bar = ((lam_bar - 1.0) / lam)[..., None] * lax.complex(b_re.astype(f32), b_im.astype(f32))
    bu = jnp.einsum('bsgi,gpi->bsgp', u.astype(jnp.complex64), b_bar)
    states = linear_scan(jnp.broadcast_to(lam_bar, bu.shape), bu)
    c_mat = lax.complex(c_re.astype(f32), c_im.astype(f32))
    y = jnp.einsum('bsgp,gip->bsgi', states, c_mat).real + d_skip.astype(f32).reshape(S5_GROUPS, S5_GROUP) * u
    y = jax.nn.gelu(y.reshape(bsz, s, D_MODEL)).astype(h.dtype)
    return y * jax.nn.sigmoid(y @ w_glu + b_glu)


def moe_swiglu(h, router, router_b, w1, w3, w2):
    logits = (h @ router).astype(jnp.float32) + router_b.astype(jnp.float32)
    top_v, top_i = lax.top_k(logits, TOP_K)
    top_w = jax.nn.softmax(top_v, axis=-1)
    gates = jnp.sum(jax.nn.one_hot(top_i, N_EXPERTS, dtype=jnp.float32) * top_w[..., None], axis=-2)
    gates = gates.astype(h.dtype)
    out = jnp.zeros_like(h)
    for e in range(N_EXPERTS):
        out = out + gates[..., e:e + 1] * swiglu(h, w1[e], w3[e], w2[e])
    return out


def setup_inputs(seed: int = 0) -> dict:
    key = jax.random.key(seed)
    ks = iter(jax.random.split(key, 64))
    f32 = jnp.float32

    def nrm(shape, scale):
        return scale * jax.random.normal(next(ks), shape, f32)

    def gain(shape):
        return 1.0 + nrm(shape, 0.02)

    x = nrm((BATCH, SEQ, D_MODEL), 1.0)
    norm_mix = gain((DEPTH, D_MODEL))
    norm_ffn = gain((DEPTH, D_MODEL))
    norm_final = gain((D_MODEL,))
    a_w_in = nrm((N_RGLRU, D_MODEL, 2 * RG_WIDTH), D_MODEL ** -0.5)
    a_conv_w = nrm((N_RGLRU, CONV_W, RG_WIDTH), CONV_W ** -0.5)
    a_conv_b = nrm((N_RGLRU, RG_WIDTH), 0.02)
    a_w_r = nrm((N_RGLRU, RG_BLOCKS, RG_BLOCK_DIM, RG_BLOCK_DIM), RG_BLOCK_DIM ** -0.5)
    a_b_r = nrm((N_RGLRU, RG_WIDTH), 0.1)
    a_w_i = nrm((N_RGLRU, RG_BLOCKS, RG_BLOCK_DIM, RG_BLOCK_DIM), RG_BLOCK_DIM ** -0.5)
    a_b_i = nrm((N_RGLRU, RG_WIDTH), 0.1)
    a0 = jax.random.uniform(next(ks), (N_RGLRU, RG_WIDTH), f32, 0.9, 0.999)
    a_lam = jnp.log(a0) - jnp.log1p(-a0)
    a_w_out = nrm((N_RGLRU, RG_WIDTH, D_MODEL), RG_WIDTH ** -0.5)
    b_w_in = nrm((N_MLSTM, D_MODEL, 2 * ML_INNER), D_MODEL ** -0.5)
    b_conv_w = nrm((N_MLSTM, CONV_W, ML_INNER), CONV_W ** -0.5)
    b_conv_b = nrm((N_MLSTM, ML_INNER), 0.02)
    b_w_q = nrm((N_MLSTM, ML_QKV_BLOCKS, ML_QKV_BLOCK, ML_QKV_BLOCK), ML_QKV_BLOCK ** -0.5)
    b_w_k = nrm((N_MLSTM, ML_QKV_BLOCKS, ML_QKV_BLOCK, ML_QKV_BLOCK), ML_QKV_BLOCK ** -0.5)
    b_w_v = nrm((N_MLSTM, ML_QKV_BLOCKS, ML_QKV_BLOCK, ML_QKV_BLOCK), ML_QKV_BLOCK ** -0.5)
    b_w_gate = nrm((N_MLSTM, 3 * ML_INNER, 2 * ML_HEADS), (3 * ML_INNER) ** -0.5)
    f_bias = jnp.broadcast_to(jnp.linspace(3.0, 6.0, ML_HEADS, dtype=f32), (N_MLSTM, ML_HEADS))
    b_b_gate = jnp.concatenate([nrm((N_MLSTM, ML_HEADS), 0.1), f_bias + nrm((N_MLSTM, ML_HEADS), 0.1)], axis=-1)
    b_norm = gain((N_MLSTM, ML_INNER))
    b_skip = gain((N_MLSTM, ML_INNER))
    b_w_out = nrm((N_MLSTM, ML_INNER, D_MODEL), ML_INNER ** -0.5)
    c_a_re = -0.5 + nrm((N_S5, S5_GROUPS, S5_STATE), 0.01)
    c_a_im = math.pi * jnp.arange(S5_STATE, dtype=f32) + nrm((N_S5, S5_GROUPS, S5_STATE), 0.01)
    c_log_step = jax.random.uniform(next(ks), (N_S5, S5_GROUPS), f32, math.log(1e-3), math.log(1e-1))
    c_b_re = nrm((N_S5, S5_GROUPS, S5_STATE, S5_GROUP), (2 * S5_GROUP) ** -0.5)
    c_b_im = nrm((N_S5, S5_GROUPS, S5_STATE, S5_GROUP), (2 * S5_GROUP) ** -0.5)
    c_c_re = nrm((N_S5, S5_GROUPS, S5_GROUP, S5_STATE), S5_STATE ** -0.5)
    c_c_im = nrm((N_S5, S5_GROUPS, S5_GROUP, S5_STATE), S5_STATE ** -0.5)
    c_d = nrm((N_S5, D_MODEL), 1.0)
    c_w_glu = nrm((N_S5, D_MODEL, D_MODEL), D_MODEL ** -0.5)
    c_b_glu = nrm((N_S5, D_MODEL), 0.02)
    f_w1 = nrm((N_DENSE, D_MODEL, D_FF), D_MODEL ** -0.5)
    f_w3 = nrm((N_DENSE, D_MODEL, D_FF), D_MODEL ** -0.5)
    f_w2 = nrm((N_DENSE, D_FF, D_MODEL), D_FF ** -0.5)
    e_router = nrm((N_MOE, D_MODEL, N_EXPERTS), D_MODEL ** -0.5)
    e_router_b = nrm((N_MOE, N_EXPERTS), 0.01)
    e_w1 = nrm((N_MOE, N_EXPERTS, D_MODEL, D_FF), D_MODEL ** -0.5)
    e_w3 = nrm((N_MOE, N_EXPERTS, D_MODEL, D_FF), D_MODEL ** -0.5)
    e_w2 = nrm((N_MOE, N_EXPERTS, D_FF, D_MODEL), D_FF ** -0.5)
    return {'x': x, 'norm_mix': norm_mix, 'norm_ffn': norm_ffn, 'norm_final': norm_final,
            'a_w_in': a_w_in, 'a_conv_w': a_conv_w, 'a_conv_b': a_conv_b, 'a_w_r': a_w_r, 'a_b_r': a_b_r,
            'a_w_i': a_w_i, 'a_b_i': a_b_i, 'a_lam': a_lam, 'a_w_out': a_w_out,
            'b_w_in': b_w_in, 'b_conv_w': b_conv_w, 'b_conv_b': b_conv_b, 'b_w_q': b_w_q, 'b_w_k': b_w_k,
            'b_w_v': b_w_v, 'b_w_gate': b_w_gate, 'b_b_gate': b_b_gate, 'b_norm': b_norm, 'b_skip': b_skip,
            'b_w_out': b_w_out,
            'c_a_re': c_a_re, 'c_a_im': c_a_im, 'c_log_step': c_log_step, 'c_b_re': c_b_re, 'c_b_im': c_b_im,
            'c_c_re': c_c_re, 'c_c_im': c_c_im, 'c_d': c_d, 'c_w_glu': c_w_glu, 'c_b_glu': c_b_glu,
            'f_w1': f_w1, 'f_w3': f_w3, 'f_w2': f_w2,
            'e_router': e_router, 'e_router_b': e_router_b, 'e_w1': e_w1, 'e_w3': e_w3, 'e_w2': e_w2}


def reference(x, norm_mix, norm_ffn, norm_final,
              a_w_in, a_conv_w, a_conv_b, a_w_r, a_b_r, a_w_i, a_b_i, a_lam, a_w_out,
              b_w_in, b_conv_w, b_conv_b, b_w_q, b_w_k, b_w_v, b_w_gate, b_b_gate, b_norm, b_skip, b_w_out,
              c_a_re, c_a_im, c_log_step, c_b_re, c_b_im, c_c_re, c_c_im, c_d, c_w_glu, c_b_glu,
              f_w1, f_w3, f_w2,
              e_router, e_router_b, e_w1, e_w3, e_w2):
    for i in range(DEPTH):
        hn = rms_norm(x, norm_mix[i])
        kind = i % N_MIXERS
        j = i // N_MIXERS
        if kind == 0:
            mix = rglru_mixer(hn, a_w_in[j], a_conv_w[j], a_conv_b[j], a_w_r[j], a_b_r[j],
                              a_w_i[j], a_b_i[j], a_lam[j], a_w_out[j])
        elif kind == 1:
            mix = mlstm_mixer(hn, b_w_in[j], b_conv_w[j], b_conv_b[j], b_w_q[j], b_w_k[j], b_w_v[j],
                              b_w_gate[j], b_b_gate[j], b_norm[j], b_skip[j], b_w_out[j])
        else:
            mix = s5_mixer(hn, c_a_re[j], c_a_im[j], c_log_step[j], c_b_re[j], c_b_im[j],
                           c_c_re[j], c_c_im[j], c_d[j], c_w_glu[j], c_b_glu[j])
        x = x + mix
        hn = rms_norm(x, norm_ffn[i])
        f = i // 2
        if i % 2 == 0:
            x = x + swiglu(hn, f_w1[f], f_w3[f], f_w2[f])
        else:
            x = x + moe_swiglu(hn, e_router[f], e_router_b[f], e_w1[f], e_w3[f], e_w2[f])
    return rms_norm(x, norm_final)
```

```python
import functools
import math

import jax
import jax.numpy as jnp
from jax import lax
from jax.experimental import pallas as pl
from jax.experimental.pallas import tpu as pltpu

F32 = jnp.float32
BF16 = jnp.bfloat16

RMS_EPS = 1e-6
CONV_W = 4
RG_C = 8.0
ML_CHUNK = 128
TOP_K = 2

V7X_VMEM_BYTES = 64 * 1024 * 1024
SUBLANES = 8
LANES = 128
MXU_DIM = 256

TOKEN_TILE = 512
GATHER_TILE = 256
GROUP_TILE = 512
S5_CHUNK = 16
S5_GROUPS_PER_STEP = 8


def _cparams(n_axes, vmem_mib):
    return pltpu.CompilerParams(
        dimension_semantics=("arbitrary",) * n_axes,
        vmem_limit_bytes=min(vmem_mib * 1024 * 1024, V7X_VMEM_BYTES - 6 * 1024 * 1024),
    )


def _const_spec(shape):
    nd = len(shape)
    return pl.BlockSpec(shape, lambda *_: (0,) * nd, pipeline_mode=pl.Buffered(1))


def _row_spec(tm, d):
    return pl.BlockSpec((tm, d), lambda i: (i, 0))


def _rms(x, g):
    return x * lax.rsqrt(jnp.mean(x * x, axis=-1, keepdims=True) + RMS_EPS) * g


def _dot(a, b):
    return jnp.dot(a, b, preferred_element_type=F32)


def _dot_nt(a, b):
    return lax.dot_general(a, b, (((1,), (1,)), ((), ())), preferred_element_type=F32)


def _dot_tn(a, b):
    return lax.dot_general(a, b, (((0,), (0,)), ((), ())), preferred_element_type=F32)


def _silu(x):
    return x * jax.nn.sigmoid(x)


def _gelu_tanh(x):
    return 0.5 * x * (1.0 + jnp.tanh(math.sqrt(2.0 / math.pi) * (x + 0.044715 * (x * x * x))))


def _log_sigmoid(x):
    return jnp.minimum(x, 0.0) - jnp.log1p(jnp.exp(-jnp.abs(x)))


def _causal_conv(cbuf_ref, tail_ref, xin, cw, cb, cols, tm):
    cbuf_ref[0:SUBLANES, :] = tail_ref[:, cols]
    cbuf_ref[SUBLANES:SUBLANES + tm, :] = xin
    out = cb + cw[0:1] * cbuf_ref[SUBLANES - 3:SUBLANES - 3 + tm, :]
    out = out + cw[1:2] * cbuf_ref[SUBLANES - 2:SUBLANES - 2 + tm, :]
    out = out + cw[2:3] * cbuf_ref[SUBLANES - 1:SUBLANES - 1 + tm, :]
    out = out + cw[3:4] * xin
    tail_ref[:, cols] = xin[tm - SUBLANES:tm, :]
    return out


def _ffn_dense_kernel(x_ref, g_ref, w1_ref, w3_ref, w2_ref, o_ref, *, ff_chunk):
    x = x_ref[...]
    hn = _rms(x, g_ref[...]).astype(BF16)
    acc = x
    for c in range(w1_ref.shape[1] // ff_chunk):
        sl = slice(c * ff_chunk, (c + 1) * ff_chunk)
        h = (_silu(_dot(hn, w1_ref[:, sl])) * _dot(hn, w3_ref[:, sl])).astype(BF16)
        acc = acc + _dot(h, w2_ref[sl, :])
    o_ref[...] = acc


def _ffn_dense(x, g, w1, w3, w2):
    s, d = x.shape
    f = w1.shape[1]
    tm = min(TOKEN_TILE, s)
    return pl.pallas_call(
        functools.partial(_ffn_dense_kernel, ff_chunk=MXU_DIM),
        out_shape=jax.ShapeDtypeStruct((s, d), F32),
        grid=(s // tm,),
        in_specs=[_row_spec(tm, d), _const_spec((1, d)), _const_spec((d, f)), _const_spec((d, f)),
                  _const_spec((f, d))],
        out_specs=_row_spec(tm, d),
        compiler_params=_cparams(1, 48),
        name="ffn_dense",
    )(x, g.reshape(1, d), w1.astype(BF16), w3.astype(BF16), w2.astype(BF16))


def _rglru_kernel(x_ref, g_ref, win_ref, cw_ref, cb_ref, wg_ref, bg_ref, nsp_ref, wout_ref, o_ref,
                  cbuf_ref, tail_ref, h_ref, *, tm, width):
    bd = MXU_DIM

    @pl.when(pl.program_id(0) == 0)
    def _():
        tail_ref[...] = jnp.zeros_like(tail_ref)
        h_ref[...] = jnp.zeros_like(h_ref)

    x = x_ref[...]
    hn = _rms(x, g_ref[...]).astype(BF16)
    row = lax.broadcasted_iota(jnp.int32, (tm, bd), 0)
    ys = []
    for n in range(width // bd):
        cols = slice(n * bd, (n + 1) * bd)
        gx = _dot(hn, win_ref[:, cols])
        xr = _dot(hn, win_ref[:, width + n * bd:width + (n + 1) * bd])
        xr = _causal_conv(cbuf_ref, tail_ref, xr, cw_ref[:, cols], cb_ref[:, cols], cols, tm)
        gates = _dot(xr.astype(BF16), wg_ref[n]) + bg_ref[n]
        r = jax.nn.sigmoid(gates[:, :bd])
        ig = jax.nn.sigmoid(gates[:, bd:])
        log_a = r * nsp_ref[:, cols]
        a = jnp.exp(log_a)
        b = jnp.sqrt(1.0 - a * a) * (ig * xr)
        d = 1
        while d < tm:
            keep = row >= d
            a_sh = jnp.where(keep, pltpu.roll(a, d, 0), 1.0)
            b_sh = jnp.where(keep, pltpu.roll(b, d, 0), 0.0)
            b = a * b_sh + b
            a = a * a_sh
            d *= 2
        h = b + a * h_ref[:, cols]
        h_ref[:, cols] = h[tm - 1:tm, :]
        ys.append((_gelu_tanh(gx) * h).astype(BF16))
    y = jnp.concatenate(ys, axis=1)
    o_ref[...] = x + _dot(y, wout_ref[...])


def _rglru_layer(x, g, w_in, conv_w, conv_b, w_r, b_r, w_i, b_i, lam, w_out):
    s, d = x.shape
    width = w_out.shape[0]
    bd = MXU_DIM
    nblk = width // bd
    tm = min(TOKEN_TILE, s)
    wg = jnp.concatenate([_block_diag_tiles(w_r, bd), _block_diag_tiles(w_i, bd)], axis=-1).astype(BF16)
    bg = jnp.concatenate([b_r.reshape(nblk, 1, bd), b_i.reshape(nblk, 1, bd)], axis=-1)
    nsp = (-RG_C * jax.nn.softplus(-lam)).reshape(1, width)
    kern = functools.partial(_rglru_kernel, tm=tm, width=width)
    return pl.pallas_call(
        kern,
        out_shape=jax.ShapeDtypeStruct((s, d), F32),
        grid=(s // tm,),
        in_specs=[_row_spec(tm, d), _const_spec((1, d)), _const_spec((d, 2 * width)),
                  _const_spec((CONV_W, width)), _const_spec((1, width)),
                  _const_spec((nblk, bd, 2 * bd)), _const_spec((nblk, 1, 2 * bd)),
                  _const_spec((1, width)), _const_spec((width, d))],
        out_specs=_row_spec(tm, d),
        scratch_shapes=[pltpu.VMEM((tm + SUBLANES, bd), F32), pltpu.VMEM((SUBLANES, width), F32),
                        pltpu.VMEM((1, width), F32)],
        compiler_params=_cparams(1, 40),
        name="rglru_layer",
    )(x, g.reshape(1, d), w_in.astype(BF16), conv_w, conv_b.reshape(1, width), wg, bg, nsp,
      w_out.astype(BF16))


def _mlstm_kernel(x_ref, g_ref, win_ref, cw_ref, cb_ref, wqk_ref, wv_ref, wgq_ref, wgk_ref, wgv_ref,
                  wgqt_ref, wgkt_ref, wgvt_ref, bg_ref, bgt_ref, ng_ref, skip_ref, wout_ref, o_ref,
                  cbuf_ref, tail_ref, q_ref, k_ref, v_ref, xc_ref, z_ref, y_ref, gcol_ref, grow_ref,
                  c_ref, n_ref, m_ref, *, tm, inner, heads):
    hd = inner // heads
    lc = ML_CHUNK
    nqb = inner // MXU_DIM
    scale = hd ** -0.5

    @pl.when(pl.program_id(0) == 0)
    def _():
        tail_ref[...] = jnp.zeros_like(tail_ref)
        c_ref[...] = jnp.zeros_like(c_ref)
        n_ref[...] = jnp.zeros_like(n_ref)
        m_ref[...] = jnp.zeros_like(m_ref)

    x = x_ref[...]
    hn = _rms(x, g_ref[...]).astype(BF16)

    for b in range(nqb):
        cols = slice(b * MXU_DIM, (b + 1) * MXU_DIM)
        xm = _dot(hn, win_ref[:, cols])
        xc = _silu(_causal_conv(cbuf_ref, tail_ref, xm, cw_ref[:, cols], cb_ref[:, cols], cols, tm))
        xcb = xc.astype(BF16)
        qk = _dot(xcb, wqk_ref[b])
        q_ref[:, cols] = qk[:, :MXU_DIM].astype(BF16)
        k_ref[:, cols] = qk[:, MXU_DIM:].astype(BF16)
        v_ref[:, cols] = _dot(xm.astype(BF16), wv_ref[b]).astype(BF16)
        xc_ref[:, cols] = xcb
        z_ref[:, cols] = _dot(hn, win_ref[:, inner + b * MXU_DIM:inner + (b + 1) * MXU_DIM]).astype(BF16)

    q = q_ref[...]
    k = k_ref[...]
    v = v_ref[...]
    gc = _dot(q, wgq_ref[...]) + _dot(k, wgk_ref[...]) + _dot(v, wgv_ref[...]) + bg_ref[...]
    lane8 = lax.broadcasted_iota(jnp.int32, gc.shape, 1)
    gcol_ref[...] = jnp.where(lane8 < heads, gc, _log_sigmoid(gc))
    gr = _dot_nt(wgqt_ref[...], q) + _dot_nt(wgkt_ref[...], k) + _dot_nt(wgvt_ref[...], v) + bgt_ref[...]
    sub8 = lax.broadcasted_iota(jnp.int32, gr.shape, 0)
    gr = jnp.where(sub8 < heads, gr, _log_sigmoid(gr))
    for c in range(tm // lc):
        grow_ref[c] = gr[:, c * lc:(c + 1) * lc]

    ti = lax.broadcasted_iota(jnp.int32, (lc, lc), 0)
    si = lax.broadcasted_iota(jnp.int32, (lc, lc), 1)
    causal = si <= ti

    def chunk_body(c, carry):
        r0 = pl.multiple_of(c * lc, lc)
        rows = pl.ds(r0, lc)
        gcol = gcol_ref[rows, :]
        grow = grow_ref[c]
        for h in range(heads):
            cols = slice(h * hd, (h + 1) * hd)
            qc = q_ref[rows, cols]
            kc = k_ref[rows, cols]
            vc = v_ref[rows, cols]
            ig_c = gcol[:, h:h + 1]
            lf_c = gcol[:, heads + h:heads + h + 1]
            ig_r = grow[h:h + 1, :]
            lf_r = grow[heads + h:heads + h + 1, :]
            bcum_c = jnp.sum(jnp.where(causal, lf_r, 0.0), axis=1, keepdims=True)
            bcum_r = jnp.sum(jnp.where(ti <= si, lf_c, 0.0), axis=0, keepdims=True)
            m_st = m_ref[h:h + 1, 0:1]
            dm = jnp.where(causal, bcum_c - bcum_r + ig_r, -jnp.inf)
            inter = bcum_c + m_st
            m_t = jnp.maximum(inter, jnp.max(dm, axis=1, keepdims=True))
            dexp = jnp.exp(dm - m_t)
            sc = jnp.exp(inter - m_t)
            sco = (_dot_nt(qc, kc) * scale) * dexp
            cst = c_ref[h]
            num = _dot(sco.astype(BF16), vc) + sc * _dot(qc, cst.astype(BF16))
            nst = n_ref[h:h + 1, :]
            qn = jnp.sum(qc.astype(F32) * nst, axis=1, keepdims=True)
            den = jnp.sum(sco, axis=1, keepdims=True) + sc * qn
            hc = num * (1.0 / jnp.maximum(jnp.abs(den), jnp.exp(-m_t)))
            m_new = m_t[lc - 1:lc, :]
            b_last = bcum_c[lc - 1:lc, :]
            w_c = jnp.exp(b_last - bcum_c + ig_c - m_new) * scale
            decay = jnp.exp(b_last + m_st - m_new)
            kw = kc.astype(F32) * w_c
            c_ref[h] = decay * cst + _dot_tn(kw.astype(BF16), vc)
            n_ref[h:h + 1, :] = decay * nst + jnp.sum(kw, axis=0, keepdims=True)
            m_ref[h:h + 1, :] = jnp.broadcast_to(m_new, (1, LANES))
            hnrm = hc * lax.rsqrt(jnp.mean(hc * hc, axis=-1, keepdims=True) + RMS_EPS) * ng_ref[:, cols]
            zc = z_ref[rows, cols].astype(F32)
            xcc = xc_ref[rows, cols].astype(F32)
            y_ref[rows, cols] = (jax.nn.sigmoid(zc) * (hnrm + skip_ref[:, cols] * xcc)).astype(BF16)
        return carry

    lax.fori_loop(0, tm // lc, chunk_body, 0)
    o_ref[...] = x + _dot(y_ref[...], wout_ref[...])


def _block_diag_tiles(w, tile):
    nb, k, _ = w.shape
    per = tile // k
    w4 = w.reshape(nb // per, per, k, k)
    eye = jnp.eye(per, dtype=w.dtype)
    return (w4[:, :, :, None, :] * eye[None, :, None, :, None]).reshape(nb // per, tile, tile)


def _mlstm_layer(x, g, w_in, conv_w, conv_b, w_q, w_k, w_v, w_gate, b_gate, norm_g, skip, w_out, heads):
    s, d = x.shape
    inner = w_out.shape[0]
    tm = min(TOKEN_TILE, s)
    ng = 2 * heads
    wqk = jnp.concatenate([_block_diag_tiles(w_q, MXU_DIM), _block_diag_tiles(w_k, MXU_DIM)],
                          axis=-1).astype(BF16)
    wv = _block_diag_tiles(w_v, MXU_DIM).astype(BF16)
    wgb = w_gate.astype(BF16)
    wg_parts = [wgb[p * inner:(p + 1) * inner] for p in range(3)]
    wgt_parts = [wp.T for wp in wg_parts]
    nqb = inner // MXU_DIM
    kern = functools.partial(_mlstm_kernel, tm=tm, inner=inner, heads=heads)
    in_specs = [_row_spec(tm, d), _const_spec((1, d)), _const_spec((d, 2 * inner)),
                _const_spec((CONV_W, inner)), _const_spec((1, inner)),
                _const_spec((nqb, MXU_DIM, 2 * MXU_DIM)), _const_spec((nqb, MXU_DIM, MXU_DIM))]
    in_specs += [_const_spec((inner, ng))] * 3 + [_const_spec((ng, inner))] * 3
    in_specs += [_const_spec((1, ng)), _const_spec((ng, 1)), _const_spec((1, inner)),
                 _const_spec((1, inner)), _const_spec((inner, d))]
    hd = inner // heads
    scratch = [pltpu.VMEM((tm + SUBLANES, MXU_DIM), F32), pltpu.VMEM((SUBLANES, inner), F32)]
    scratch += [pltpu.VMEM((tm, inner), BF16)] * 6
    scratch += [pltpu.VMEM((tm, ng), F32), pltpu.VMEM((tm // ML_CHUNK, ng, ML_CHUNK), F32),
                pltpu.VMEM((heads, hd, hd), F32), pltpu.VMEM((SUBLANES, hd), F32),
                pltpu.VMEM((SUBLANES, LANES), F32)]
    return pl.pallas_call(
        kern,
        out_shape=jax.ShapeDtypeStruct((s, d), F32),
        grid=(s // tm,),
        in_specs=in_specs,
        out_specs=_row_spec(tm, d),
        scratch_shapes=scratch,
        compiler_params=_cparams(1, 56),
        name="mlstm_layer",
    )(x, g.reshape(1, d), w_in.astype(BF16), conv_w, conv_b.reshape(1, inner), wqk, wv,
      *wg_parts, *wgt_parts, b_gate.reshape(1, ng), b_gate.reshape(ng, 1), norm_g.reshape(1, inner),
      skip.reshape(1, inner), w_out.astype(BF16))


def _norm_kernel(x_ref, g_ref, o_ref):
    o_ref[...] = _rms(x_ref[...], g_ref[...]).astype(o_ref.dtype)


def _norm(x, g, dtype):
    s, d = x.shape
    tm = min(TOKEN_TILE, s)
    return pl.pallas_call(
        _norm_kernel,
        out_shape=jax.ShapeDtypeStruct((s, d), dtype),
        grid=(s // tm,),
        in_specs=[_row_spec(tm, d), _const_spec((1, d))],
        out_specs=_row_spec(tm, d),
        compiler_params=_cparams(1, 16),
        name="rms_norm",
    )(x, g.reshape(1, d))


def _s5_scan_kernel(u_ref, t_ref, bre_ref, bim_ref, cre_ref, cim_ref, pre_ref, pim_ref, y_ref,
                    sre_ref, sim_ref, *, gb, nchunk, pstate):
    for g in range(gb):
        u = u_ref[g]
        y_ref[g] = _dot(u, t_ref[g])
        sre_ref[:, g * pstate:(g + 1) * pstate] = _dot(u, bre_ref[g])
        sim_ref[:, g * pstate:(g + 1) * pstate] = _dot(u, bim_ref[g])
    xre = sre_ref[...]
    xim = sim_ref[...]
    row = lax.broadcasted_iota(jnp.int32, xre.shape, 0)
    d, kstep = 1, 0
    while d < nchunk:
        keep = row >= d
        re_sh = jnp.where(keep, pltpu.roll(xre, d, 0), 0.0)
        im_sh = jnp.where(keep, pltpu.roll(xim, d, 0), 0.0)
        pr = pre_ref[kstep]
        pi = pim_ref[kstep]
        xre, xim = xre + (pr * re_sh - pi * im_sh), xim + (pr * im_sh + pi * re_sh)
        d *= 2
        kstep += 1
    keep = row >= 1
    pre = jnp.where(keep, pltpu.roll(xre, 1, 0), 0.0).astype(BF16)
    pim = jnp.where(keep, pltpu.roll(xim, 1, 0), 0.0).astype(BF16)
    for g in range(gb):
        sl = slice(g * pstate, (g + 1) * pstate)
        y_ref[g] = y_ref[g] + _dot(pre[:, sl], cre_ref[g]) + _dot(pim[:, sl], cim_ref[g])


def _cmul(ar, ai, br, bi):
    return ar * br - ai * bi, ar * bi + ai * br


def _s5_operators(a_re, a_im, log_step, b_re, b_im, c_re, c_im, nchunk):
    hi = lax.Precision.HIGHEST
    lc = S5_CHUNK
    step = jnp.exp(log_step)[:, None]
    mag = jnp.exp(a_re * step)
    lr, li = mag * jnp.cos(a_im * step), mag * jnp.sin(a_im * step)
    den = a_re * a_re + a_im * a_im
    fr, fi = _cmul(lr - 1.0, li, a_re / den, -a_im / den)
    bbr, bbi = _cmul(fr[..., None], fi[..., None], b_re, b_im)
    pows = [(jnp.ones_like(lr), jnp.zeros_like(li))]
    for _ in range(lc):
        pows.append(_cmul(pows[-1][0], pows[-1][1], lr, li))
    pr = jnp.stack([p[0] for p in pows], axis=1)
    pi = jnp.stack([p[1] for p in pows], axis=1)
    mr, mi = _cmul(c_re[:, None], c_im[:, None], pr[:, :, None, :], pi[:, :, None, :])
    kk = (jnp.einsum('gljp,gpi->glij', mr[:, :lc], bbr, precision=hi)
          - jnp.einsum('gljp,gpi->glij', mi[:, :lc], bbi, precision=hi))
    lag = jnp.arange(lc)[None, :] - jnp.arange(lc)[:, None]
    kt = jnp.take(kk, jnp.clip(lag, 0, lc - 1), axis=1)
    kt = jnp.where((lag >= 0)[None, :, :, None, None], kt, 0.0)
    g_, i_, j_ = kk.shape[0], kk.shape[2], kk.shape[3]
    toep = kt.transpose(0, 1, 3, 2, 4).reshape(g_, lc * i_, lc * j_)
    rev_r, rev_i = pr[:, lc - 1::-1][:, :lc], pi[:, lc - 1::-1][:, :lc]
    bcr, bci = _cmul(rev_r[:, :, None, :], rev_i[:, :, None, :],
                     bbr.transpose(0, 2, 1)[:, None], bbi.transpose(0, 2, 1)[:, None])
    p_ = lr.shape[1]
    bcr, bci = bcr.reshape(g_, lc * i_, p_), bci.reshape(g_, lc * i_, p_)
    cr = mr[:, 1:lc + 1].transpose(0, 3, 1, 2).reshape(g_, p_, lc * j_)
    ci = -mi[:, 1:lc + 1].transpose(0, 3, 1, 2).reshape(g_, p_, lc * j_)
    nsteps = max(1, int(math.ceil(math.log2(max(nchunk, 2)))))
    sr, si = pr[:, lc], pi[:, lc]
    scan_r, scan_i = [], []
    for _ in range(nsteps):
        scan_r.append(sr.reshape(1, -1))
        scan_i.append(si.reshape(1, -1))
        sr, si = _cmul(sr, si, sr, si)
    return (toep.astype(BF16), bcr.astype(BF16), bci.astype(BF16), cr.astype(BF16), ci.astype(BF16),
            jnp.stack(scan_r), jnp.stack(scan_i))


def _s5_out_kernel(x_ref, yt_ref, g_ref, d_ref, wglu_ref, bglu_ref, o_ref):
    x = x_ref[...]
    hn = _rms(x, g_ref[...])
    y = _gelu_tanh(yt_ref[...] + d_ref[...] * hn)
    o_ref[...] = x + y * jax.nn.sigmoid(_dot(y.astype(BF16), wglu_ref[...]) + bglu_ref[...])


def _s5_layer(x, g, a_re, a_im, log_step, b_re, b_im, c_re, c_im, d_skip, w_glu, b_glu):
    s, d = x.shape
    ngroups, pstate, gch = b_re.shape
    lc = S5_CHUNK
    nchunk = s // lc
    gb = S5_GROUPS_PER_STEP
    toep, bcr, bci, cr, ci, scan_r, scan_i = _s5_operators(a_re, a_im, log_step, b_re, b_im, c_re, c_im,
                                                           nchunk)
    nsteps = scan_r.shape[0]
    hn = _norm(x, g, BF16)
    u = hn.reshape(nchunk, lc, ngroups, gch).transpose(2, 0, 1, 3).reshape(ngroups, nchunk, lc * gch)
    w = lc * gch
    grp = lambda *shape: pl.BlockSpec((gb,) + shape, lambda i: (i,) + (0,) * len(shape))
    yt = pl.pallas_call(
        functools.partial(_s5_scan_kernel, gb=gb, nchunk=nchunk, pstate=pstate),
        out_shape=jax.ShapeDtypeStruct((ngroups, nchunk, w), F32),
        grid=(ngroups // gb,),
        in_specs=[grp(nchunk, w), grp(w, w), grp(w, pstate), grp(w, pstate), grp(pstate, w), grp(pstate, w),
                  pl.BlockSpec((nsteps, 1, gb * pstate), lambda i: (0, 0, i)),
                  pl.BlockSpec((nsteps, 1, gb * pstate), lambda i: (0, 0, i))],
        out_specs=grp(nchunk, w),
        scratch_shapes=[pltpu.VMEM((nchunk, gb * pstate), F32), pltpu.VMEM((nchunk, gb * pstate), F32)],
        compiler_params=_cparams(1, 48),
        name="s5_scan",
    )(u, toep, bcr, bci, cr, ci, scan_r, scan_i)
    y = yt.reshape(ngroups, nchunk, lc, gch).transpose(1, 2, 0, 3).reshape(s, d)
    tm = min(TOKEN_TILE, s)
    return pl.pallas_call(
        _s5_out_kernel,
        out_shape=jax.ShapeDtypeStruct((s, d), F32),
        grid=(s // tm,),
        in_specs=[_row_spec(tm, d), _row_spec(tm, d), _const_spec((1, d)), _const_spec((1, d)),
                  _const_spec((d, d)), _const_spec((1, d))],
        out_specs=_row_spec(tm, d),
        compiler_params=_cparams(1, 32),
        name="s5_out",
    )(x, y, g.reshape(1, d), d_skip.reshape(1, d), w_glu.astype(BF16), b_glu.reshape(1, d))


def _router_kernel(x_ref, g_ref, wr_ref, br_ref, hn_ref, meta_ref, cnt_ref, run_ref, *, tm, ne):
    @pl.when(pl.program_id(0) == 0)
    def _():
        run_ref[...] = jnp.zeros_like(run_ref)

    hn = _rms(x_ref[...], g_ref[...])
    hn_ref[...] = hn
    logits = jnp.dot(hn, wr_ref[...], preferred_element_type=F32, precision=lax.Precision.HIGHEST)
    logits = logits + br_ref[...]
    lane = lax.broadcasted_iota(jnp.int32, (tm, ne), 1).astype(F32)
    m1 = jnp.max(logits, axis=1, keepdims=True)
    i1 = jnp.min(jnp.where(logits == m1, lane, float(ne)), axis=1, keepdims=True)
    sel1 = lane == i1
    rest = jnp.where(sel1, -jnp.inf, logits)
    m2 = jnp.max(rest, axis=1, keepdims=True)
    i2 = jnp.min(jnp.where(rest == m2, lane, float(ne)), axis=1, keepdims=True)
    sel2 = lane == i2
    e = jnp.exp(m2 - m1)
    w1 = 1.0 / (1.0 + e)
    w2 = e / (1.0 + e)
    sel = jnp.logical_or(sel1, sel2)
    ti = lax.broadcasted_iota(jnp.int32, (tm, tm), 0)
    si = lax.broadcasted_iota(jnp.int32, (tm, tm), 1)
    earlier = jnp.where(si < ti, 1.0, 0.0).astype(BF16)
    excl = _dot(earlier, jnp.where(sel, 1.0, 0.0).astype(BF16)) + run_ref[...]
    r1 = jnp.sum(jnp.where(sel1, excl, 0.0), axis=1, keepdims=True)
    r2 = jnp.sum(jnp.where(sel2, excl, 0.0), axis=1, keepdims=True)
    total = run_ref[...] + jnp.sum(jnp.where(sel, 1.0, 0.0), axis=0, keepdims=True)
    run_ref[...] = total
    cnt_ref[...] = total
    meta = jnp.zeros((tm, ne), F32)
    for c, val in enumerate([i1, i2, r1, r2, w1, w2]):
        meta = jnp.where(lane == float(c), val, meta)
    meta_ref[...] = meta


def _dispatch_kernel(pos_ref, hn_ref, xs_in_ref, xs_ref, sem, *, tg):
    del xs_in_ref

    def row_copy(t, dst_row):
        return pltpu.make_async_copy(hn_ref.at[pl.ds(t, 1), :], xs_ref.at[pl.ds(dst_row, 1), :], sem)

    def issue(t, carry):
        row_copy(t, pos_ref[0, 0, t]).start()
        row_copy(t, pos_ref[0, 0, tg + t]).start()
        return carry

    lax.fori_loop(0, tg, issue, 0, unroll=8)

    def drain(t, carry):
        row_copy(t, 0).wait()
        row_copy(t, 0).wait()
        return carry

    lax.fori_loop(0, tg, drain, 0, unroll=8)


def _expert_kernel(te_ref, nv_ref, xs_ref, w1_ref, w3_ref, w2_ref, ys_ref):
    i = pl.program_id(0)
    j = pl.program_id(1)

    @pl.when(i < nv_ref[0])
    def _():
        xb = xs_ref[...].astype(BF16)
        h = (_silu(_dot(xb, w1_ref[0])) * _dot(xb, w3_ref[0])).astype(BF16)
        part = _dot(h, w2_ref[0])

        @pl.when(j == 0)
        def _():
            ys_ref[...] = part

        @pl.when(j != 0)
        def _():
            ys_ref[...] = ys_ref[...] + part

    @pl.when(i >= nv_ref[0])
    def _():
        ys_ref[...] = jnp.zeros_like(ys_ref)


def _combine_kernel(pos_ref, x_ref, meta_ref, gf_ref, ys_ref, o_ref, buf_ref, sem, *, tg, final_norm):
    def row_copy(t, k, src_row):
        return pltpu.make_async_copy(ys_ref.at[pl.ds(src_row, 1), :], buf_ref.at[k, pl.ds(t, 1), :], sem)

    def issue(t, carry):
        row_copy(t, 0, pos_ref[0, 0, t]).start()
        row_copy(t, 1, pos_ref[0, 0, tg + t]).start()
        return carry

    lax.fori_loop(0, tg, issue, 0, unroll=8)

    def drain(t, carry):
        row_copy(t, 0, 0).wait()
        row_copy(t, 1, 0).wait()
        return carry

    lax.fori_loop(0, tg, drain, 0, unroll=8)

    meta = meta_ref[...]
    out = x_ref[...] + (meta[:, 4:5] * buf_ref[0] + meta[:, 5:6] * buf_ref[1])
    if final_norm:
        out = _rms(out, gf_ref[...])
    o_ref[...] = out


def _moe_layer(x, g, router, router_b, w1, w3, w2, g_final, final_norm):
    s, d = x.shape
    ne = router.shape[1]
    f = w1.shape[2]
    tm = min(TOKEN_TILE, s)
    tg = min(GATHER_TILE, s)
    tr = GROUP_TILE
    hn, meta, cnt = pl.pallas_call(
        functools.partial(_router_kernel, tm=tm, ne=ne),
        out_shape=(jax.ShapeDtypeStruct((s, d), F32), jax.ShapeDtypeStruct((s, ne), F32),
                   jax.ShapeDtypeStruct((1, ne), F32)),
        grid=(s // tm,),
        in_specs=[_row_spec(tm, d), _const_spec((1, d)), _const_spec((d, ne)), _const_spec((1, ne))],
        out_specs=(_row_spec(tm, d), _row_spec(tm, ne), pl.BlockSpec((1, ne), lambda i: (0, 0))),
        scratch_shapes=[pltpu.VMEM((1, ne), F32)],
        compiler_params=_cparams(1, 24),
        name="moe_router",
    )(x, g.reshape(1, d), router, router_b.reshape(1, ne))

    counts = cnt[0].astype(jnp.int32)
    padded = ((counts + tr - 1) // tr) * tr
    ends = jnp.cumsum(padded)
    starts = ends - padded
    e1, e2 = meta[:, 0].astype(jnp.int32), meta[:, 1].astype(jnp.int32)
    p1 = starts[e1] + meta[:, 2].astype(jnp.int32)
    p2 = starts[e2] + meta[:, 3].astype(jnp.int32)
    pos = jnp.concatenate([p1.reshape(s // tg, 1, tg), p2.reshape(s // tg, 1, tg)], axis=-1)
    n_tiles = (TOP_K * s) // tr + ne
    rows = n_tiles * tr
    n_valid = (ends[-1] // tr).astype(jnp.int32)
    tile_row = jnp.minimum(jnp.arange(n_tiles, dtype=jnp.int32), n_valid - 1) * tr
    tile_expert = jnp.minimum(jnp.searchsorted(ends, tile_row, side='right'), ne - 1).astype(jnp.int32)

    pos_spec = pl.BlockSpec((1, 1, TOP_K * tg), lambda i: (i, 0, 0), memory_space=pltpu.SMEM)
    any_spec = pl.BlockSpec(memory_space=pl.ANY)
    xs = pl.pallas_call(
        functools.partial(_dispatch_kernel, tg=tg),
        out_shape=jax.ShapeDtypeStruct((rows, d), F32),
        grid=(s // tg,),
        in_specs=[pos_spec, _row_spec(tg, d), any_spec],
        out_specs=any_spec,
        scratch_shapes=[pltpu.SemaphoreType.DMA],
        input_output_aliases={2: 0},
        compiler_params=_cparams(1, 16),
        name="moe_dispatch",
    )(pos, hn, jnp.zeros((rows, d), F32))

    tf = f // 2
    ys = pl.pallas_call(
        _expert_kernel,
        out_shape=jax.ShapeDtypeStruct((rows, d), F32),
        grid_spec=pltpu.PrefetchScalarGridSpec(
            num_scalar_prefetch=2,
            grid=(n_tiles, f // tf),
            in_specs=[pl.BlockSpec((tr, d), lambda i, j, te, nv: (i, 0)),
                      pl.BlockSpec((1, d, tf), lambda i, j, te, nv: (te[i], 0, j)),
                      pl.BlockSpec((1, d, tf), lambda i, j, te, nv: (te[i], 0, j)),
                      pl.BlockSpec((1, tf, d), lambda i, j, te, nv: (te[i], j, 0))],
            out_specs=pl.BlockSpec((tr, d), lambda i, j, te, nv: (i, 0)),
        ),
        compiler_params=_cparams(2, 48),
        name="moe_experts",
    )(tile_expert, n_valid.reshape(1), xs, w1.astype(BF16), w3.astype(BF16), w2.astype(BF16))

    return pl.pallas_call(
        functools.partial(_combine_kernel, tg=tg, final_norm=final_norm),
        out_shape=jax.ShapeDtypeStruct((s, d), F32),
        grid=(s // tg,),
        in_specs=[pos_spec, _row_spec(tg, d), _row_spec(tg, ne), _const_spec((1, d)), any_spec],
        out_specs=_row_spec(tg, d),
        scratch_shapes=[pltpu.VMEM((TOP_K, tg, d), F32), pltpu.SemaphoreType.DMA],
        compiler_params=_cparams(1, 24),
        name="moe_combine",
    )(pos, x, meta, g_final.reshape(1, d), ys)


def kernel(x, norm_mix, norm_ffn, norm_final, a_w_in, a_conv_w, a_conv_b, a_w_r, a_b_r, a_w_i, a_b_i, a_lam, a_w_out, b_w_in, b_conv_w, b_conv_b, b_w_q, b_w_k, b_w_v, b_w_gate, b_b_gate, b_norm, b_skip, b_w_out, c_a_re, c_a_im, c_log_step, c_b_re, c_b_im, c_c_re, c_c_im, c_d, c_w_glu, c_b_glu, f_w1, f_w3, f_w2, e_router, e_router_b, e_w1, e_w3, e_w2):
    bsz, seq, d = x.shape
    depth = norm_mix.shape[0]
    heads = b_b_gate.shape[1] // 2
    outs = []
    for bi in range(bsz):
        h = x[bi]
        for i in range(depth):
            kind, j = i % 3, i // 3
            if kind == 0:
                h = _rglru_layer(h, norm_mix[i], a_w_in[j], a_conv_w[j], a_conv_b[j], a_w_r[j], a_b_r[j],
                                 a_w_i[j], a_b_i[j], a_lam[j], a_w_out[j])
            elif kind == 1:
                h = _mlstm_layer(h, norm_mix[i], b_w_in[j], b_conv_w[j], b_conv_b[j], b_w_q[j], b_w_k[j],
                                 b_w_v[j], b_w_gate[j], b_b_gate[j], b_norm[j], b_skip[j], b_w_out[j], heads)
            else:
                h = _s5_layer(h, norm_mix[i], c_a_re[j], c_a_im[j], c_log_step[j], c_b_re[j], c_b_im[j],
                              c_c_re[j], c_c_im[j], c_d[j], c_w_glu[j], c_b_glu[j])
            fidx = i // 2
            last = i == depth - 1
            if i % 2 == 0:
                h = _ffn_dense(h, norm_ffn[i], f_w1[fidx], f_w3[fidx], f_w2[fidx])
                if last:
                    h = _norm(h, norm_final, F32)
            else:
                h = _moe_layer(h, norm_ffn[i], e_router[fidx], e_router_b[fidx], e_w1[fidx], e_w3[fidx],
                               e_w2[fidx], norm_final, last)
        outs.append(h)
    return jnp.stack(outs)
```

```python
import functools
import math

import jax
import jax.numpy as jnp
from jax import lax
from jax.experimental import pallas as pl
from jax.experimental.pallas import tpu as pltpu

F32 = jnp.float32
BF16 = jnp.bfloat16
I32 = jnp.int32

RMS_EPS = 1e-6
CONV_W = 4
RG_C = 8.0
ML_CHUNK = 128
TOP_K = 2

V7X_VMEM_BYTES = 64 * 1024 * 1024
SUBLANES = 8
LANES = 128
MXU_DIM = 256

TOKEN_TILE = 512
GATHER_TILE = 256
GROUP_TILE = 512
S5_CHUNK = 16
S5_CHUNK_BLOCK = 512


def _cparams(n_axes, vmem_mib):
    return pltpu.CompilerParams(
        dimension_semantics=("arbitrary",) * n_axes,
        vmem_limit_bytes=min(vmem_mib * 1024 * 1024, V7X_VMEM_BYTES - 6 * 1024 * 1024),
    )


def _const_spec(shape):
    nd = len(shape)
    return pl.BlockSpec(shape, lambda *_: (0,) * nd, pipeline_mode=pl.Buffered(1))


def _layer_spec(stacked, layer):
    nd = stacked.ndim
    return pl.BlockSpec((None,) + tuple(stacked.shape[1:]), lambda *_: (layer,) + (0,) * (nd - 1),
                        pipeline_mode=pl.Buffered(1))


def _row_spec(tm, d):
    return pl.BlockSpec((tm, d), lambda i: (i, 0))


def _rms(x, g):
    return x * lax.rsqrt(jnp.mean(x * x, axis=-1, keepdims=True) + RMS_EPS) * g


def _dot(a, b):
    return jnp.dot(a, b, preferred_element_type=F32)


def _dot_nt(a, b):
    return lax.dot_general(a, b, (((1,), (1,)), ((), ())), preferred_element_type=F32)


def _dot_tn(a, b):
    return lax.dot_general(a, b, (((0,), (0,)), ((), ())), preferred_element_type=F32)


def _silu(x):
    return x * jax.nn.sigmoid(x)


def _gelu_tanh(x):
    return 0.5 * x * (1.0 + jnp.tanh(math.sqrt(2.0 / math.pi) * (x + 0.044715 * (x * x * x))))


def _log_sigmoid(x):
    return jnp.minimum(x, 0.0) - jnp.log1p(jnp.exp(-jnp.abs(x)))


def _causal_conv(tail_ref, xin, cw, cb, cols, tm):
    prev = tail_ref[:, cols]
    sub = lax.broadcasted_iota(I32, prev.shape, 0)
    out = cb
    for d in (3, 2, 1):
        sh = pltpu.roll(xin, d, 0)
        head = jnp.where(sub < d, pltpu.roll(prev, d, 0), sh[0:SUBLANES])
        out = out + cw[3 - d:4 - d] * jnp.concatenate([head, sh[SUBLANES:]], axis=0)
    out = out + cw[3:4] * xin
    tail_ref[:, cols] = xin[tm - SUBLANES:tm, :]
    return out


def _block_diag_tiles(w, tile):
    nb, k, _ = w.shape
    per = tile // k
    w4 = w.reshape(nb // per, per, k, k)
    eye = jnp.eye(per, dtype=w.dtype)
    return (w4[:, :, :, None, :] * eye[None, :, None, :, None]).reshape(nb // per, tile, tile)


def _ffn_dense_kernel(x_ref, g_ref, w1_ref, w3_ref, w2_ref, o_ref, *, ff_chunk):
    x = x_ref[...]
    hn = _rms(x, g_ref[...]).astype(BF16)
    acc = x
    for c in range(w1_ref.shape[1] // ff_chunk):
        sl = slice(c * ff_chunk, (c + 1) * ff_chunk)
        h = (_silu(_dot(hn, w1_ref[:, sl])) * _dot(hn, w3_ref[:, sl])).astype(BF16)
        acc = acc + _dot(h, w2_ref[sl, :])
    o_ref[...] = acc


def _ffn_dense(x, g, w1, w3, w2, layer):
    s, d = x.shape
    tm = min(TOKEN_TILE, s)
    return pl.pallas_call(
        functools.partial(_ffn_dense_kernel, ff_chunk=MXU_DIM),
        out_shape=jax.ShapeDtypeStruct((s, d), F32),
        grid=(s // tm,),
        in_specs=[_row_spec(tm, d), _const_spec((1, d)), _layer_spec(w1, layer), _layer_spec(w3, layer),
                  _layer_spec(w2, layer)],
        out_specs=_row_spec(tm, d),
        compiler_params=_cparams(1, 48),
        name="ffn_dense",
    )(x, g.reshape(1, d), w1, w3, w2)


def _linear_scan_rows(a, b, h0):
    tm, c = a.shape
    ngrp = tm // SUBLANES
    a3 = a.reshape(ngrp, SUBLANES, c)
    b3 = b.reshape(ngrp, SUBLANES, c)
    sub = lax.broadcasted_iota(I32, (1, SUBLANES, c), 1)
    for d in (1, 2, 4):
        keep = sub >= d
        a_sh = jnp.where(keep, pltpu.roll(a3, d, 1), 1.0)
        b_sh = jnp.where(keep, pltpu.roll(b3, d, 1), 0.0)
        b3 = a3 * b_sh + b3
        a3 = a3 * a_sh
    carry = h0
    rows = []
    for j in range(ngrp):
        hj = b3[j] + a3[j] * carry
        rows.append(hj)
        carry = hj[SUBLANES - 1:SUBLANES, :]
    return jnp.concatenate(rows, axis=0), carry


def _rglru_kernel(x_ref, g_ref, win_ref, cw_ref, cb_ref, wg_ref, bg_ref, nsp_ref, wout_ref, o_ref,
                  tail_ref, h_ref, *, tm, width):
    bd = MXU_DIM

    @pl.when(pl.program_id(0) == 0)
    def _():
        tail_ref[...] = jnp.zeros_like(tail_ref)
        h_ref[...] = jnp.zeros_like(h_ref)

    x = x_ref[...]
    hn = _rms(x, g_ref[...]).astype(BF16)
    ys = []
    for n in range(width // bd):
        cols = slice(n * bd, (n + 1) * bd)
        gx = _dot(hn, win_ref[:, cols])
        xr = _dot(hn, win_ref[:, width + n * bd:width + (n + 1) * bd])
        xr = _causal_conv(tail_ref, xr, cw_ref[:, cols], cb_ref[:, cols], cols, tm)
        gates = _dot(xr.astype(BF16), wg_ref[n]) + bg_ref[n]
        r = jax.nn.sigmoid(gates[:, :bd])
        ig = jax.nn.sigmoid(gates[:, bd:])
        a = jnp.exp(r * nsp_ref[:, cols])
        v = 1.0 - a * a
        b = jnp.where(v > 0.0, v * lax.rsqrt(v), 0.0) * (ig * xr)
        h, last = _linear_scan_rows(a, b, h_ref[:, cols])
        h_ref[:, cols] = last
        ys.append((_gelu_tanh(gx) * h).astype(BF16))
    y = jnp.concatenate(ys, axis=1)
    o_ref[...] = x + _dot(y, wout_ref[...])


def _rglru_layer(x, g, w_in, conv_w, conv_b, w_r, b_r, w_i, b_i, lam, w_out, layer):
    s, d = x.shape
    width = w_out.shape[1]
    bd = MXU_DIM
    nblk = width // bd
    tm = min(TOKEN_TILE, s)
    wg = jnp.concatenate([_block_diag_tiles(w_r, bd), _block_diag_tiles(w_i, bd)], axis=-1).astype(BF16)
    bg = jnp.concatenate([b_r.reshape(nblk, 1, bd), b_i.reshape(nblk, 1, bd)], axis=-1)
    nsp = (-RG_C * jax.nn.softplus(-lam)).reshape(1, width)
    kern = functools.partial(_rglru_kernel, tm=tm, width=width)
    return pl.pallas_call(
        kern,
        out_shape=jax.ShapeDtypeStruct((s, d), F32),
        grid=(s // tm,),
        in_specs=[_row_spec(tm, d), _const_spec((1, d)), _layer_spec(w_in, layer),
                  _const_spec((CONV_W, width)), _const_spec((1, width)),
                  _const_spec((nblk, bd, 2 * bd)), _const_spec((nblk, 1, 2 * bd)),
                  _const_spec((1, width)), _layer_spec(w_out, layer)],
        out_specs=_row_spec(tm, d),
        scratch_shapes=[pltpu.VMEM((SUBLANES, width), F32), pltpu.VMEM((1, width), F32)],
        compiler_params=_cparams(1, 40),
        name="rglru_layer",
    )(x, g.reshape(1, d), w_in, conv_w, conv_b.reshape(1, width), wg, bg, nsp, w_out)


def _mlstm_kernel(x_ref, g_ref, win_ref, cw_ref, cb_ref, wqk_ref, wv_ref, wgq_ref, wgk_ref, wgv_ref,
                  bg_ref, ng_ref, skip_ref, wout_ref, o_ref,
                  tail_ref, q_ref, k_ref, v_ref, xc_ref, z_ref, y_ref, gcol_ref, grow_ref,
                  c_ref, n_ref, m_ref, *, tm, inner, heads):
    hd = inner // heads
    lc = ML_CHUNK
    nqb = inner // MXU_DIM
    scale = hd ** -0.5

    @pl.when(pl.program_id(0) == 0)
    def _():
        tail_ref[...] = jnp.zeros_like(tail_ref)
        c_ref[...] = jnp.zeros_like(c_ref)
        n_ref[...] = jnp.zeros_like(n_ref)
        m_ref[...] = jnp.zeros_like(m_ref)

    x = x_ref[...]
    hn = _rms(x, g_ref[...]).astype(BF16)

    for b in range(nqb):
        cols = slice(b * MXU_DIM, (b + 1) * MXU_DIM)
        xm = _dot(hn, win_ref[:, cols])
        xc = _silu(_causal_conv(tail_ref, xm, cw_ref[:, cols], cb_ref[:, cols], cols, tm))
        xcb = xc.astype(BF16)
        qk = _dot(xcb, wqk_ref[b])
        q_ref[:, cols] = qk[:, :MXU_DIM].astype(BF16)
        k_ref[:, cols] = qk[:, MXU_DIM:].astype(BF16)
        v_ref[:, cols] = _dot(xm.astype(BF16), wv_ref[b]).astype(BF16)
        xc_ref[:, cols] = xcb
        z_ref[:, cols] = _dot(hn, win_ref[:, inner + b * MXU_DIM:inner + (b + 1) * MXU_DIM]).astype(BF16)

    gc = (_dot(q_ref[...], wgq_ref[...]) + _dot(k_ref[...], wgk_ref[...]) + _dot(v_ref[...], wgv_ref[...])
          + bg_ref[...])
    lane = lax.broadcasted_iota(I32, gc.shape, 1)
    gc = jnp.where(lane < heads, gc, _log_sigmoid(gc))
    gcol_ref[...] = gc
    gr = gc.T
    for c in range(tm // lc):
        grow_ref[c] = gr[0:SUBLANES, c * lc:(c + 1) * lc]

    ti = lax.broadcasted_iota(I32, (lc, lc), 0)
    si = lax.broadcasted_iota(I32, (lc, lc), 1)
    causal = si <= ti

    def chunk_body(c, carry):
        r0 = pl.multiple_of(c * lc, lc)
        rows = pl.ds(r0, lc)
        gcol = gcol_ref[rows, :]
        grow = grow_ref[c]
        nb = n_ref[...].astype(BF16)
        for h in range(heads):
            cols = slice(h * hd, (h + 1) * hd)
            qc = q_ref[rows, cols]
            kc = k_ref[rows, cols]
            vc = v_ref[rows, cols]
            ig_c = gcol[:, h:h + 1]
            ig_r = grow[h:h + 1, :]
            lf_c = gcol[:, heads + h:heads + h + 1]
            lf_r = grow[heads + h:heads + h + 1, :]
            bcum_c = jnp.sum(jnp.where(causal, lf_r, 0.0), axis=1, keepdims=True)
            bcum_r = jnp.sum(jnp.where(ti <= si, lf_c, 0.0), axis=0, keepdims=True)
            m_st = m_ref[h:h + 1, 0:1]
            dm = jnp.where(causal, bcum_c - bcum_r + ig_r, -jnp.inf)
            inter = bcum_c + m_st
            m_t = jnp.maximum(inter, jnp.max(dm, axis=1, keepdims=True))
            dexp = jnp.exp(dm - m_t)
            sc = jnp.exp(inter - m_t)
            sco = (_dot_nt(qc, kc) * scale) * dexp
            cst = c_ref[h]
            num = _dot(sco.astype(BF16), vc) + sc * _dot(qc, cst.astype(BF16))
            qn = _dot_nt(qc, nb)[:, h:h + 1]
            den = jnp.sum(sco, axis=1, keepdims=True) + sc * qn
            hc = num * (1.0 / jnp.maximum(jnp.abs(den), jnp.exp(-m_t)))
            m_new = m_t[lc - 1:lc, :]
            b_last = bcum_c[lc - 1:lc, :]
            w_c = jnp.exp(b_last - bcum_c + ig_c - m_new) * scale
            w_r = jnp.exp(b_last - bcum_r + ig_r - m_new) * scale
            decay = jnp.exp(b_last + m_st - m_new)
            kw = kc * w_c.astype(BF16)
            c_ref[h] = decay * cst + _dot_tn(kw, vc)
            wk = _dot(jnp.broadcast_to(w_r, (SUBLANES, lc)).astype(BF16), kc)
            n_ref[h:h + 1, :] = decay * n_ref[h:h + 1, :] + wk[0:1, :]
            m_ref[h:h + 1, :] = jnp.broadcast_to(m_new, (1, LANES))
            hnrm = hc * lax.rsqrt(jnp.mean(hc * hc, axis=-1, keepdims=True) + RMS_EPS) * ng_ref[:, cols]
            zc = z_ref[rows, cols].astype(F32)
            xcc = xc_ref[rows, cols].astype(F32)
            y_ref[rows, cols] = (jax.nn.sigmoid(zc) * (hnrm + skip_ref[:, cols] * xcc)).astype(BF16)
        return carry

    lax.fori_loop(0, tm // lc, chunk_body, 0)
    o_ref[...] = x + _dot(y_ref[...], wout_ref[...])


def _mlstm_layer(x, g, w_in, conv_w, conv_b, w_q, w_k, w_v, w_gate, b_gate, norm_g, skip, w_out, heads,
                 layer):
    s, d = x.shape
    inner = w_out.shape[1]
    tm = min(TOKEN_TILE, s)
    ng = 2 * heads
    wqk = jnp.concatenate([_block_diag_tiles(w_q, MXU_DIM), _block_diag_tiles(w_k, MXU_DIM)],
                          axis=-1).astype(BF16)
    wv = _block_diag_tiles(w_v, MXU_DIM).astype(BF16)
    wgp = jnp.pad(w_gate, ((0, 0), (0, LANES - ng))).astype(BF16)
    wg_parts = [wgp[p * inner:(p + 1) * inner] for p in range(3)]
    bgp = jnp.pad(b_gate, (0, LANES - ng)).reshape(1, LANES)
    nqb = inner // MXU_DIM
    kern = functools.partial(_mlstm_kernel, tm=tm, inner=inner, heads=heads)
    in_specs = [_row_spec(tm, d), _const_spec((1, d)), _layer_spec(w_in, layer),
                _const_spec((CONV_W, inner)), _const_spec((1, inner)),
                _const_spec((nqb, MXU_DIM, 2 * MXU_DIM)), _const_spec((nqb, MXU_DIM, MXU_DIM))]
    in_specs += [_const_spec((inner, LANES))] * 3
    in_specs += [_const_spec((1, LANES)), _const_spec((1, inner)), _const_spec((1, inner)),
                 _layer_spec(w_out, layer)]
    hd = inner // heads
    scratch = [pltpu.VMEM((SUBLANES, inner), F32)]
    scratch += [pltpu.VMEM((tm, inner), BF16)] * 6
    scratch += [pltpu.VMEM((tm, LANES), F32), pltpu.VMEM((tm // ML_CHUNK, SUBLANES, ML_CHUNK), F32),
                pltpu.VMEM((heads, hd, hd), F32), pltpu.VMEM((SUBLANES, hd), F32),
                pltpu.VMEM((SUBLANES, LANES), F32)]
    return pl.pallas_call(
        kern,
        out_shape=jax.ShapeDtypeStruct((s, d), F32),
        grid=(s // tm,),
        in_specs=in_specs,
        out_specs=_row_spec(tm, d),
        scratch_shapes=scratch,
        compiler_params=_cparams(1, 56),
        name="mlstm_layer",
    )(x, g.reshape(1, d), w_in, conv_w, conv_b.reshape(1, inner), wqk, wv, *wg_parts, bgp,
      norm_g.reshape(1, inner), skip.reshape(1, inner), w_out)


def _s5_in_kernel(x_ref, g_ref, o_ref, slab_ref, *, tm):
    hn = _rms(x_ref[...], g_ref[...])
    nslab = hn.shape[1] // LANES
    for k in range(nslab):
        slab_ref[k] = hn[:, k * LANES:(k + 1) * LANES]
    for s in range(S5_CHUNK):
        for k in range(nslab):
            rows = slab_ref[k, pl.ds(s, tm // S5_CHUNK, stride=S5_CHUNK), :]
            o_ref[s, :, k * LANES:(k + 1) * LANES] = rows.astype(BF16)


def _s5_scan_kernel(xs_ref, t_ref, bc_ref, cc_ref, pre_ref, pim_ref, ys_ref,
                    sre_ref, sim_ref, cre_ref, cim_ref, *, cb, half):
    lc = S5_CHUNK

    @pl.when(pl.program_id(1) == 0)
    def _():
        cre_ref[...] = jnp.zeros_like(cre_ref)
        cim_ref[...] = jnp.zeros_like(cim_ref)

    u = jnp.concatenate([xs_ref[s] for s in range(lc)], axis=1)
    npair = (lc * LANES) // MXU_DIM
    ytiles = []
    for b in range(npair):
        acc = _dot(u[:, 0:MXU_DIM], t_ref[b])
        for a in range(1, b + 1):
            acc = acc + _dot(u[:, a * MXU_DIM:(a + 1) * MXU_DIM], t_ref[b - a])
        ytiles.append(acc)
    sinc = _dot(u, bc_ref[...])
    sre_ref[...] = sinc[:, :half]
    sim_ref[...] = sinc[:, half:]
    cin_re, cin_im = cre_ref[...], cim_ref[...]
    pr0, pi0 = pre_ref[0], pim_ref[0]
    sre_ref[0:1, :] = sre_ref[0:1, :] + (pr0 * cin_re - pi0 * cin_im)
    sim_ref[0:1, :] = sim_ref[0:1, :] + (pr0 * cin_im + pi0 * cin_re)
    xre, xim = sre_ref[...], sim_ref[...]
    row = lax.broadcasted_iota(I32, xre.shape, 0)
    d, kstep = 1, 0
    while d < cb:
        keep = row >= d
        re_sh = jnp.where(keep, pltpu.roll(xre, d, 0), 0.0)
        im_sh = jnp.where(keep, pltpu.roll(xim, d, 0), 0.0)
        pr, pi = pre_ref[kstep], pim_ref[kstep]
        xre, xim = xre + (pr * re_sh - pi * im_sh), xim + (pr * im_sh + pi * re_sh)
        d *= 2
        kstep += 1
    cre_ref[...] = xre[cb - 1:cb, :]
    cim_ref[...] = xim[cb - 1:cb, :]
    first = row == 0
    prev_re = jnp.where(first, cin_re, pltpu.roll(xre, 1, 0)).astype(BF16)
    prev_im = jnp.where(first, cin_im, pltpu.roll(xim, 1, 0)).astype(BF16)
    prev = jnp.concatenate([prev_re, prev_im], axis=1)
    for b in range(npair):
        yb = ytiles[b] + _dot(prev, cc_ref[:, b * MXU_DIM:(b + 1) * MXU_DIM])
        ys_ref[2 * b] = yb[:, :LANES]
        ys_ref[2 * b + 1] = yb[:, LANES:]


def _cmul(ar, ai, br, bi):
    return ar * br - ai * bi, ar * bi + ai * br


def _s5_operators(a_re, a_im, log_step, b_re, b_im, c_re, c_im, cb):
    hi = lax.Precision.HIGHEST
    lc = S5_CHUNK
    ng, np_, ni = b_re.shape
    gpv = LANES // ni
    nv = ng // gpv
    step = jnp.exp(log_step)[:, None]
    mag = jnp.exp(a_re * step)
    lr, li = mag * jnp.cos(a_im * step), mag * jnp.sin(a_im * step)
    den = a_re * a_re + a_im * a_im
    fr, fi = _cmul(lr - 1.0, li, a_re / den, -a_im / den)
    bbr, bbi = _cmul(fr[..., None], fi[..., None], b_re, b_im)
    pows = [(jnp.ones_like(lr), jnp.zeros_like(li))]
    for _ in range(lc):
        pows.append(_cmul(pows[-1][0], pows[-1][1], lr, li))
    pr = jnp.stack([p[0] for p in pows], axis=1)
    pi = jnp.stack([p[1] for p in pows], axis=1)
    mr, mi = _cmul(c_re[:, None], c_im[:, None], pr[:, :, None, :], pi[:, :, None, :])
    kk = (jnp.einsum('gljp,gpi->glij', mr[:, :lc], bbr, precision=hi)
          - jnp.einsum('gljp,gpi->glij', mi[:, :lc], bbi, precision=hi))
    eye = jnp.eye(gpv, dtype=F32)
    npair = lc // 2
    zero_lag = jnp.zeros_like(kk[:, 0])
    kd = jnp.stack([jnp.stack([jnp.stack([kk[:, 2 * dd + t2 - s2] if 2 * dd + t2 - s2 >= 0 else zero_lag
                                          for t2 in range(2)], axis=1)
                               for s2 in range(2)], axis=1)
                    for dd in range(npair)], axis=1)
    kd = kd.reshape(nv, gpv, npair, 2, 2, ni, ni).transpose(0, 2, 3, 1, 5, 4, 6)
    toep = (kd[:, :, :, :, :, :, None, :] * eye[None, None, None, :, None, None, :, None])
    toep = toep.reshape(nv, npair, 2 * gpv * ni, 2 * gpv * ni)
    rev_r = jnp.stack([pows[lc - 1 - s_][0] for s_ in range(lc)], axis=1)
    rev_i = jnp.stack([pows[lc - 1 - s_][1] for s_ in range(lc)], axis=1)
    bcr, bci = _cmul(rev_r[:, :, None, :], rev_i[:, :, None, :],
                     bbr.transpose(0, 2, 1)[:, None], bbi.transpose(0, 2, 1)[:, None])

    def rows_sgi(m):
        m = m.reshape(nv, gpv, lc, ni, np_).transpose(0, 2, 1, 3, 4)
        m = m[:, :, :, :, None, :] * eye[None, None, :, None, :, None]
        return m.reshape(nv, lc * gpv * ni, gpv * np_)

    bc = jnp.concatenate([rows_sgi(bcr), rows_sgi(bci)], axis=-1)
    def cols_tgj(m):
        m = m.reshape(nv, gpv, lc, ni, np_).transpose(0, 1, 4, 2, 3)
        m = m[:, :, :, :, None, :] * eye[None, :, None, None, :, None]
        return m.reshape(nv, gpv * np_, lc * gpv * ni)

    cc = jnp.concatenate([cols_tgj(mr[:, 1:lc + 1]), cols_tgj(-mi[:, 1:lc + 1])], axis=1)
    nsteps = max(1, int(math.ceil(math.log2(max(cb, 2)))))
    sr, si = pr[:, lc], pi[:, lc]
    scan_r, scan_i = [], []
    for _ in range(nsteps):
        scan_r.append(sr.reshape(1, -1))
        scan_i.append(si.reshape(1, -1))
        sr, si = _cmul(sr, si, sr, si)
    return toep.astype(BF16), bc.astype(BF16), cc.astype(BF16), jnp.stack(scan_r), jnp.stack(scan_i)


def _s5_out_kernel(x_ref, ys_ref, g_ref, d_ref, wglu_ref, bglu_ref, o_ref, slab_ref, *, tm):
    x = x_ref[...]
    nslab = x.shape[1] // LANES
    for s in range(S5_CHUNK):
        for k in range(nslab):
            slab_ref[k, pl.ds(s, tm // S5_CHUNK, stride=S5_CHUNK), :] = ys_ref[s, :, k * LANES:(k + 1) * LANES]
    yt = jnp.concatenate([slab_ref[k] for k in range(nslab)], axis=1)
    hn = _rms(x, g_ref[...])
    y = _gelu_tanh(yt + d_ref[...] * hn)
    o_ref[...] = x + y * jax.nn.sigmoid(_dot(y.astype(BF16), wglu_ref[...]) + bglu_ref[...])


def _s5_layer(x, g, a_re, a_im, log_step, b_re, b_im, c_re, c_im, d_skip, w_glu, b_glu, layer):
    s, d = x.shape
    ngroups, pstate, gch = b_re.shape
    lc = S5_CHUNK
    nchunk = s // lc
    cb = min(S5_CHUNK_BLOCK, nchunk)
    gpv = LANES // gch
    nv = ngroups // gpv
    half = gpv * pstate
    toep, bc, cc, scan_r, scan_i = _s5_operators(a_re, a_im, log_step, b_re, b_im, c_re, c_im, cb)
    nsteps = scan_r.shape[0]
    tm = min(TOKEN_TILE, s)
    nslab = d // LANES
    step_major = pl.BlockSpec((lc, tm // lc, d), lambda i: (0, i, 0))
    xs = pl.pallas_call(
        functools.partial(_s5_in_kernel, tm=tm),
        out_shape=jax.ShapeDtypeStruct((lc, nchunk, d), BF16),
        grid=(s // tm,),
        in_specs=[_row_spec(tm, d), _const_spec((1, d))],
        out_specs=step_major,
        scratch_shapes=[pltpu.VMEM((nslab, tm, LANES), F32)],
        compiler_params=_cparams(1, 24),
        name="s5_in",
    )(x, g.reshape(1, d))
    wl = lc * LANES
    npair = lc // 2
    ys = pl.pallas_call(
        functools.partial(_s5_scan_kernel, cb=cb, half=half),
        out_shape=jax.ShapeDtypeStruct((lc, nchunk, d), F32),
        grid=(nv, nchunk // cb),
        in_specs=[pl.BlockSpec((lc, cb, LANES), lambda v, j: (0, j, v)),
                  pl.BlockSpec((None, npair, MXU_DIM, MXU_DIM), lambda v, j: (v, 0, 0, 0)),
                  pl.BlockSpec((None, wl, 2 * half), lambda v, j: (v, 0, 0)),
                  pl.BlockSpec((None, 2 * half, wl), lambda v, j: (v, 0, 0)),
                  pl.BlockSpec((nsteps, 1, half), lambda v, j: (0, 0, v)),
                  pl.BlockSpec((nsteps, 1, half), lambda v, j: (0, 0, v))],
        out_specs=pl.BlockSpec((lc, cb, LANES), lambda v, j: (0, j, v)),
        scratch_shapes=[pltpu.VMEM((cb, half), F32), pltpu.VMEM((cb, half), F32),
                        pltpu.VMEM((1, half), F32), pltpu.VMEM((1, half), F32)],
        compiler_params=_cparams(2, 52),
        name="s5_scan",
    )(xs, toep, bc, cc, scan_r, scan_i)
    return pl.pallas_call(
        functools.partial(_s5_out_kernel, tm=tm),
        out_shape=jax.ShapeDtypeStruct((s, d), F32),
        grid=(s // tm,),
        in_specs=[_row_spec(tm, d), step_major, _const_spec((1, d)), _const_spec((1, d)),
                  _layer_spec(w_glu, layer), _const_spec((1, d))],
        out_specs=_row_spec(tm, d),
        scratch_shapes=[pltpu.VMEM((nslab, tm, LANES), F32)],
        compiler_params=_cparams(1, 32),
        name="s5_out",
    )(x, ys, g.reshape(1, d), d_skip.reshape(1, d), w_glu, b_glu.reshape(1, d))


def _router_kernel(x_ref, g_ref, wr_ref, br_ref, hn_ref, wts_ref, idx_ref, cnt_ref, run_ref, *, tm, ne):
    @pl.when(pl.program_id(0) == 0)
    def _():
        run_ref[...] = jnp.zeros_like(run_ref)

    hn = _rms(x_ref[...], g_ref[...])
    hn_ref[...] = hn
    logits = jnp.dot(hn, wr_ref[...], preferred_element_type=F32, precision=lax.Precision.HIGHEST)
    logits = logits + br_ref[...]
    lane = lax.broadcasted_iota(I32, (tm, ne), 1).astype(F32)
    m1 = jnp.max(logits, axis=1, keepdims=True)
    i1 = jnp.min(jnp.where(logits == m1, lane, float(ne)), axis=1, keepdims=True)
    sel1 = lane == i1
    rest = jnp.where(sel1, -jnp.inf, logits)
    m2 = jnp.max(rest, axis=1, keepdims=True)
    i2 = jnp.min(jnp.where(rest == m2, lane, float(ne)), axis=1, keepdims=True)
    sel2 = lane == i2
    e = jnp.exp(m2 - m1)
    w1 = 1.0 / (1.0 + e)
    w2 = e / (1.0 + e)
    sel = jnp.logical_or(sel1, sel2)
    ti = lax.broadcasted_iota(I32, (tm, tm), 0)
    si = lax.broadcasted_iota(I32, (tm, tm), 1)
    earlier = jnp.where(si < ti, 1.0, 0.0).astype(BF16)
    excl = _dot(earlier, jnp.where(sel, 1.0, 0.0).astype(BF16)) + run_ref[...]
    r1 = jnp.sum(jnp.where(sel1, excl, 0.0), axis=1, keepdims=True)
    r2 = jnp.sum(jnp.where(sel2, excl, 0.0), axis=1, keepdims=True)
    total = run_ref[...] + jnp.sum(jnp.where(sel, 1.0, 0.0), axis=0, keepdims=True)
    run_ref[...] = total
    cnt_ref[...] = total
    wts_ref[...] = jnp.where(lane == 0.0, w1, jnp.where(lane == 1.0, w2, 0.0))
    lane_w = lax.broadcasted_iota(I32, (tm, LANES), 1)
    meta = jnp.zeros((tm, LANES), F32)
    for c, val in enumerate([i1, i2, r1, r2]):
        meta = jnp.where(lane_w == c, val, meta)
    idx_ref[...] = meta.T[0:SUBLANES, :].astype(I32)


def _sorted_row(starts_ref, idx_ref, k, t):
    return starts_ref[idx_ref[k, t]] + idx_ref[TOP_K + k, t]


def _dispatch_kernel(starts_ref, ends_ref, idx_ref, hn_ref, xs_ref, zero_ref, sem, zsem, *, tg, tr, ne):
    @pl.when(pl.program_id(0) == 0)
    def _():
        zero_ref[...] = jnp.zeros_like(zero_ref)
        last_tile = xs_ref.shape[0] - tr
        for e in range(ne):
            for r0 in (jnp.maximum(ends_ref[e] - tr, 0), jnp.minimum(ends_ref[ne - 1] + e * tr, last_tile)):
                cp = pltpu.make_async_copy(zero_ref, xs_ref.at[pl.ds(pl.multiple_of(r0, tr), tr), :], zsem)
                cp.start()
                cp.wait()

    def row_copy(t, dst_row):
        return pltpu.make_async_copy(hn_ref.at[pl.ds(t, 1), :], xs_ref.at[pl.ds(dst_row, 1), :], sem)

    def issue(t, carry):
        for k in range(TOP_K):
            row_copy(t, _sorted_row(starts_ref, idx_ref, k, t)).start()
        return carry

    lax.fori_loop(0, tg, issue, 0, unroll=8)

    def drain(t, carry):
        for k in range(TOP_K):
            row_copy(t, 0).wait()
        return carry

    lax.fori_loop(0, tg, drain, 0, unroll=8)


def _expert_kernel(te_ref, nv_ref, xs_ref, w1_ref, w3_ref, w2_ref, ys_ref):
    i = pl.program_id(0)
    j = pl.program_id(1)

    @pl.when(i < nv_ref[0])
    def _():
        xb = xs_ref[...].astype(BF16)
        h = (_silu(_dot(xb, w1_ref[...])) * _dot(xb, w3_ref[...])).astype(BF16)
        part = _dot(h, w2_ref[...])

        @pl.when(j == 0)
        def _():
            ys_ref[...] = part

        @pl.when(j != 0)
        def _():
            ys_ref[...] = ys_ref[...] + part

    @pl.when(i >= nv_ref[0])
    def _():
        ys_ref[...] = jnp.zeros_like(ys_ref)


def _combine_kernel(starts_ref, idx_ref, x_ref, wts_ref, gf_ref, ys_ref, o_ref, buf_ref, sem, *, tg,
                    final_norm):
    def row_copy(t, k, src_row):
        return pltpu.make_async_copy(ys_ref.at[pl.ds(src_row, 1), :], buf_ref.at[k, pl.ds(t, 1), :], sem)

    def issue(t, carry):
        for k in range(TOP_K):
            row_copy(t, k, _sorted_row(starts_ref, idx_ref, k, t)).start()
        return carry

    lax.fori_loop(0, tg, issue, 0, unroll=8)

    def drain(t, carry):
        for k in range(TOP_K):
            row_copy(t, k, 0).wait()
        return carry

    lax.fori_loop(0, tg, drain, 0, unroll=8)

    wts = wts_ref[...]
    out = x_ref[...] + (wts[:, 0:1] * buf_ref[0] + wts[:, 1:2] * buf_ref[1])
    if final_norm:
        out = _rms(out, gf_ref[...])
    o_ref[...] = out


def _moe_layer(x, g, router, router_b, w1, w3, w2, layer, g_final, final_norm):
    s, d = x.shape
    ne = router.shape[1]
    f = w1.shape[3]
    tm = min(TOKEN_TILE, s)
    tg = min(GATHER_TILE, s)
    tr = GROUP_TILE
    hn, wts, idx, cnt = pl.pallas_call(
        functools.partial(_router_kernel, tm=tm, ne=ne),
        out_shape=(jax.ShapeDtypeStruct((s, d), F32), jax.ShapeDtypeStruct((s, ne), F32),
                   jax.ShapeDtypeStruct((SUBLANES, s), I32), jax.ShapeDtypeStruct((1, ne), F32)),
        grid=(s // tm,),
        in_specs=[_row_spec(tm, d), _const_spec((1, d)), _const_spec((d, ne)), _const_spec((1, ne))],
        out_specs=(_row_spec(tm, d), _row_spec(tm, ne), pl.BlockSpec((SUBLANES, tm), lambda i: (0, i)),
                   pl.BlockSpec((1, ne), lambda i: (0, 0))),
        scratch_shapes=[pltpu.VMEM((1, ne), F32)],
        compiler_params=_cparams(1, 24),
        name="moe_router",
    )(x, g.reshape(1, d), router, router_b.reshape(1, ne))

    counts = cnt[0].astype(I32)
    padded = ((counts + tr - 1) // tr) * tr
    ends = jnp.cumsum(padded)
    starts = ends - padded
    n_tiles = (TOP_K * s) // tr + ne
    rows = n_tiles * tr
    n_valid = ends[-1] // tr
    tile_row = jnp.minimum(jnp.arange(n_tiles, dtype=I32), n_valid - 1) * tr
    tile_expert = jnp.minimum(jnp.sum((ends[None, :] <= tile_row[:, None]).astype(I32), axis=1), ne - 1)

    idx_spec = pl.BlockSpec((SUBLANES, tg), lambda i, *_: (0, i), memory_space=pltpu.SMEM)
    any_spec = pl.BlockSpec(memory_space=pl.ANY)
    xs = pl.pallas_call(
        functools.partial(_dispatch_kernel, tg=tg, tr=tr, ne=ne),
        out_shape=jax.ShapeDtypeStruct((rows, d), F32),
        grid_spec=pltpu.PrefetchScalarGridSpec(
            num_scalar_prefetch=2,
            grid=(s // tg,),
            in_specs=[idx_spec, pl.BlockSpec((tg, d), lambda i, *_: (i, 0))],
            out_specs=any_spec,
            scratch_shapes=[pltpu.VMEM((tr, d), F32), pltpu.SemaphoreType.DMA, pltpu.SemaphoreType.DMA],
        ),
        compiler_params=_cparams(1, 16),
        name="moe_dispatch",
    )(starts, ends, idx, hn)

    tf = f // 2
    ys = pl.pallas_call(
        _expert_kernel,
        out_shape=jax.ShapeDtypeStruct((rows, d), F32),
        grid_spec=pltpu.PrefetchScalarGridSpec(
            num_scalar_prefetch=2,
            grid=(n_tiles, f // tf),
            in_specs=[pl.BlockSpec((tr, d), lambda i, j, te, nv: (jnp.minimum(i, nv[0] - 1), 0)),
                      pl.BlockSpec((None, None, d, tf), lambda i, j, te, nv: (layer, te[i], 0, j)),
                      pl.BlockSpec((None, None, d, tf), lambda i, j, te, nv: (layer, te[i], 0, j)),
                      pl.BlockSpec((None, None, tf, d), lambda i, j, te, nv: (layer, te[i], j, 0))],
            out_specs=pl.BlockSpec((tr, d), lambda i, j, te, nv: (i, 0)),
        ),
        compiler_params=_cparams(2, 48),
        name="moe_experts",
    )(tile_expert, n_valid.reshape(1), xs, w1, w3, w2)

    return pl.pallas_call(
        functools.partial(_combine_kernel, tg=tg, final_norm=final_norm),
        out_shape=jax.ShapeDtypeStruct((s, d), F32),
        grid_spec=pltpu.PrefetchScalarGridSpec(
            num_scalar_prefetch=1,
            grid=(s // tg,),
            in_specs=[idx_spec, pl.BlockSpec((tg, d), lambda i, *_: (i, 0)),
                      pl.BlockSpec((tg, ne), lambda i, *_: (i, 0)),
                      pl.BlockSpec((1, d), lambda i, *_: (0, 0)), any_spec],
            out_specs=pl.BlockSpec((tg, d), lambda i, *_: (i, 0)),
            scratch_shapes=[pltpu.VMEM((TOP_K, tg, d), F32), pltpu.SemaphoreType.DMA],
        ),
        compiler_params=_cparams(1, 24),
        name="moe_combine",
    )(starts, idx, x, wts, g_final.reshape(1, d), ys)


def _norm_kernel(x_ref, g_ref, o_ref):
    o_ref[...] = _rms(x_ref[...], g_ref[...])


def _final_norm(x, g):
    s, d = x.shape
    tm = min(TOKEN_TILE, s)
    return pl.pallas_call(
        _norm_kernel,
        out_shape=jax.ShapeDtypeStruct((s, d), F32),
        grid=(s // tm,),
        in_specs=[_row_spec(tm, d), _const_spec((1, d))],
        out_specs=_row_spec(tm, d),
        compiler_params=_cparams(1, 16),
        name="final_norm",
    )(x, g.reshape(1, d))


def kernel(x, norm_mix, norm_ffn, norm_final, a_w_in, a_conv_w, a_conv_b, a_w_r, a_b_r, a_w_i, a_b_i, a_lam, a_w_out, b_w_in, b_conv_w, b_conv_b, b_w_q, b_w_k, b_w_v, b_w_gate, b_b_gate, b_norm, b_skip, b_w_out, c_a_re, c_a_im, c_log_step, c_b_re, c_b_im, c_c_re, c_c_im, c_d, c_w_glu, c_b_glu, f_w1, f_w3, f_w2, e_router, e_router_b, e_w1, e_w3, e_w2):
    bsz, seq, d = x.shape
    depth = norm_mix.shape[0]
    heads = b_b_gate.shape[1] // 2
    a_w_in, a_w_out, b_w_in, b_w_out, c_w_glu = (w.astype(BF16) for w in (a_w_in, a_w_out, b_w_in, b_w_out,
                                                                            c_w_glu))
    f_w1, f_w3, f_w2, e_w1, e_w3, e_w2 = (w.astype(BF16) for w in (f_w1, f_w3, f_w2, e_w1, e_w3, e_w2))
    outs = []
    for bi in range(bsz):
        h = x[bi]
        for i in range(depth):
            kind, j = i % 3, i // 3
            if kind == 0:
                h = _rglru_layer(h, norm_mix[i], a_w_in, a_conv_w[j], a_conv_b[j], a_w_r[j], a_b_r[j],
                                 a_w_i[j], a_b_i[j], a_lam[j], a_w_out, j)
            elif kind == 1:
                h = _mlstm_layer(h, norm_mix[i], b_w_in, b_conv_w[j], b_conv_b[j], b_w_q[j], b_w_k[j],
                                 b_w_v[j], b_w_gate[j], b_b_gate[j], b_norm[j], b_skip[j], b_w_out, heads, j)
            else:
                h = _s5_layer(h, norm_mix[i], c_a_re[j], c_a_im[j], c_log_step[j], c_b_re[j], c_b_im[j],
                              c_c_re[j], c_c_im[j], c_d[j], c_w_glu, c_b_glu[j], j)
            fidx = i // 2
            last = i == depth - 1
            if i % 2 == 0:
                h = _ffn_dense(h, norm_ffn[i], f_w1, f_w3, f_w2, fidx)
                if last:
                    h = _final_norm(h, norm_final)
            else:
                h = _moe_layer(h, norm_ffn[i], e_router[fidx], e_router_b[fidx], e_w1, e_w3, e_w2, fidx,
                               norm_final, last)
        outs.append(h)
    return jnp.stack(outs)
```

```python
import functools
import math

import jax
import jax.numpy as jnp
from jax import lax
from jax.experimental import pallas as pl
from jax.experimental.pallas import tpu as pltpu

F32 = jnp.float32
BF16 = jnp.bfloat16
I32 = jnp.int32

RMS_EPS = 1e-6
CONV_W = 4
RG_C = 8.0
ML_CHUNK = 128
TOP_K = 2

V7X_VMEM_BYTES = 64 * 1024 * 1024
SUBLANES = 8
LANES = 128
MXU_DIM = 256

TOKEN_TILE = 512
GATHER_TILE = 256
GROUP_TILE = 512
S5_CHUNK = 16
S5_CHUNK_BLOCK = 512


def _cparams(n_axes, vmem_mib):
    return pltpu.CompilerParams(
        dimension_semantics=("arbitrary",) * n_axes,
        vmem_limit_bytes=min(vmem_mib * 1024 * 1024, V7X_VMEM_BYTES - 6 * 1024 * 1024),
    )


def _const_spec(shape):
    nd = len(shape)
    return pl.BlockSpec(shape, lambda *_: (0,) * nd, pipeline_mode=pl.Buffered(1))


def _layer_spec(stacked, layer):
    nd = stacked.ndim
    return pl.BlockSpec((None,) + tuple(stacked.shape[1:]), lambda *_: (layer,) + (0,) * (nd - 1),
                        pipeline_mode=pl.Buffered(1))


def _row_spec(tm, d):
    return pl.BlockSpec((tm, d), lambda i: (i, 0))


def _rms(x, g):
    return x * lax.rsqrt(jnp.mean(x * x, axis=-1, keepdims=True) + RMS_EPS) * g


def _dot(a, b):
    return jnp.dot(a, b, preferred_element_type=F32)


def _dot_nt(a, b):
    return lax.dot_general(a, b, (((1,), (1,)), ((), ())), preferred_element_type=F32)


def _dot_tn(a, b):
    return lax.dot_general(a, b, (((0,), (0,)), ((), ())), preferred_element_type=F32)


def _silu(x):
    return x * jax.nn.sigmoid(x)


def _gelu_tanh(x):
    return 0.5 * x * (1.0 + jnp.tanh(math.sqrt(2.0 / math.pi) * (x + 0.044715 * (x * x * x))))


def _log_sigmoid(x):
    return jnp.minimum(x, 0.0) - jnp.log1p(jnp.exp(-jnp.abs(x)))


def _causal_conv(tail_ref, xin, cw, cb, cols, tm):
    prev = tail_ref[:, cols]
    sub = lax.broadcasted_iota(I32, prev.shape, 0)
    out = cb
    for d in (3, 2, 1):
        sh = pltpu.roll(xin, d, 0)
        head = jnp.where(sub < d, pltpu.roll(prev, d, 0), sh[0:SUBLANES])
        out = out + cw[3 - d:4 - d] * jnp.concatenate([head, sh[SUBLANES:]], axis=0)
    out = out + cw[3:4] * xin
    tail_ref[:, cols] = xin[tm - SUBLANES:tm, :]
    return out


def _spread_masked(x, spread, row_group, col_group):
    y = jnp.dot(x.astype(BF16), spread.astype(BF16), preferred_element_type=F32)
    keep = row_group(jnp.arange(x.shape[0]))[:, None] == col_group(jnp.arange(spread.shape[1]))[None, :]
    return jnp.where(keep, y, 0.0).astype(BF16)


def _block_diag_tiles(w, tile):
    nb, k, _ = w.shape
    per = tile // k
    spread = jnp.tile(jnp.eye(k, dtype=F32), (1, per))
    tiles = _spread_masked(w.reshape(nb * k, k), spread, lambda r: (r // k) % per, lambda c: c // k)
    return tiles.reshape(nb // per, tile, tile)


def _ffn_dense_kernel(x_ref, g_ref, w1_ref, w3_ref, w2_ref, o_ref, *, ff_chunk):
    x = x_ref[...]
    hn = _rms(x, g_ref[...]).astype(BF16)
    acc = x
    for c in range(w1_ref.shape[1] // ff_chunk):
        sl = slice(c * ff_chunk, (c + 1) * ff_chunk)
        h = (_silu(_dot(hn, w1_ref[:, sl])) * _dot(hn, w3_ref[:, sl])).astype(BF16)
        acc = acc + _dot(h, w2_ref[sl, :])
    o_ref[...] = acc


def _ffn_dense(x, g, w1, w3, w2, layer):
    s, d = x.shape
    tm = min(TOKEN_TILE, s)
    return pl.pallas_call(
        functools.partial(_ffn_dense_kernel, ff_chunk=MXU_DIM),
        out_shape=jax.ShapeDtypeStruct((s, d), F32),
        grid=(s // tm,),
        in_specs=[_row_spec(tm, d), _const_spec((1, d)), _layer_spec(w1, layer), _layer_spec(w3, layer),
                  _layer_spec(w2, layer)],
        out_specs=_row_spec(tm, d),
        compiler_params=_cparams(1, 48),
        name="ffn_dense",
    )(x, g.reshape(1, d), w1, w3, w2)


def _linear_scan_rows(a, b, h0):
    tm, c = a.shape
    ngrp = tm // SUBLANES
    a3 = a.reshape(ngrp, SUBLANES, c)
    b3 = b.reshape(ngrp, SUBLANES, c)
    sub = lax.broadcasted_iota(I32, (1, SUBLANES, c), 1)
    for d in (1, 2, 4):
        keep = sub >= d
        a_sh = jnp.where(keep, pltpu.roll(a3, d, 1), 1.0)
        b_sh = jnp.where(keep, pltpu.roll(b3, d, 1), 0.0)
        b3 = a3 * b_sh + b3
        a3 = a3 * a_sh
    carry = h0
    rows = []
    for j in range(ngrp):
        hj = b3[j] + a3[j] * carry
        rows.append(hj)
        carry = hj[SUBLANES - 1:SUBLANES, :]
    return jnp.concatenate(rows, axis=0), carry


def _rglru_kernel(x_ref, g_ref, win_ref, cw_ref, cb_ref, wg_ref, bg_ref, nsp_ref, wout_ref, o_ref,
                  tail_ref, h_ref, *, tm, width):
    bd = MXU_DIM

    @pl.when(pl.program_id(0) == 0)
    def _():
        tail_ref[...] = jnp.zeros_like(tail_ref)
        h_ref[...] = jnp.zeros_like(h_ref)

    x = x_ref[...]
    hn = _rms(x, g_ref[...]).astype(BF16)
    ys = []
    for n in range(width // bd):
        cols = slice(n * bd, (n + 1) * bd)
        gx = _dot(hn, win_ref[:, cols])
        xr = _dot(hn, win_ref[:, width + n * bd:width + (n + 1) * bd])
        xr = _causal_conv(tail_ref, xr, cw_ref[:, cols], cb_ref[:, cols], cols, tm)
        gates = _dot(xr.astype(BF16), wg_ref[n]) + bg_ref[n]
        r = jax.nn.sigmoid(gates[:, :bd])
        ig = jax.nn.sigmoid(gates[:, bd:])
        a = jnp.exp(r * nsp_ref[:, cols])
        v = 1.0 - a * a
        b = jnp.where(v > 0.0, v * lax.rsqrt(v), 0.0) * (ig * xr)
        h, last = _linear_scan_rows(a, b, h_ref[:, cols])
        h_ref[:, cols] = last
        ys.append((_gelu_tanh(gx) * h).astype(BF16))
    y = jnp.concatenate(ys, axis=1)
    o_ref[...] = x + _dot(y, wout_ref[...])


def _rglru_layer(x, g, w_in, conv_w, conv_b, w_r, b_r, w_i, b_i, lam, w_out, layer):
    s, d = x.shape
    width = w_out.shape[1]
    bd = MXU_DIM
    nblk = width // bd
    tm = min(TOKEN_TILE, s)
    wg = jnp.concatenate([_block_diag_tiles(w_r, bd), _block_diag_tiles(w_i, bd)], axis=-1).astype(BF16)
    bg = jnp.concatenate([b_r.reshape(nblk, 1, bd), b_i.reshape(nblk, 1, bd)], axis=-1)
    nsp = (-RG_C * jax.nn.softplus(-lam)).reshape(1, width)
    kern = functools.partial(_rglru_kernel, tm=tm, width=width)
    return pl.pallas_call(
        kern,
        out_shape=jax.ShapeDtypeStruct((s, d), F32),
        grid=(s // tm,),
        in_specs=[_row_spec(tm, d), _const_spec((1, d)), _layer_spec(w_in, layer),
                  _const_spec((CONV_W, width)), _const_spec((1, width)),
                  _const_spec((nblk, bd, 2 * bd)), _const_spec((nblk, 1, 2 * bd)),
                  _const_spec((1, width)), _layer_spec(w_out, layer)],
        out_specs=_row_spec(tm, d),
        scratch_shapes=[pltpu.VMEM((SUBLANES, width), F32), pltpu.VMEM((1, width), F32)],
        compiler_params=_cparams(1, 40),
        name="rglru_layer",
    )(x, g.reshape(1, d), w_in, conv_w, conv_b.reshape(1, width), wg, bg, nsp, w_out)


def _mlstm_kernel(x_ref, g_ref, win_ref, cw_ref, cb_ref, wqk_ref, wv_ref, wgq_ref, wgk_ref, wgv_ref,
                  bg_ref, ng_ref, skip_ref, wout_ref, o_ref,
                  tail_ref, q_ref, k_ref, v_ref, xc_ref, z_ref, y_ref, gcol_ref, grow_ref,
                  c_ref, n_ref, m_ref, *, tm, inner, heads):
    hd = inner // heads
    lc = ML_CHUNK
    nqb = inner // MXU_DIM
    scale = hd ** -0.5

    @pl.when(pl.program_id(0) == 0)
    def _():
        tail_ref[...] = jnp.zeros_like(tail_ref)
        c_ref[...] = jnp.zeros_like(c_ref)
        n_ref[...] = jnp.zeros_like(n_ref)
        m_ref[...] = jnp.zeros_like(m_ref)

    x = x_ref[...]
    hn = _rms(x, g_ref[...]).astype(BF16)

    for b in range(nqb):
        cols = slice(b * MXU_DIM, (b + 1) * MXU_DIM)
        xm = _dot(hn, win_ref[:, cols])
        xc = _silu(_causal_conv(tail_ref, xm, cw_ref[:, cols], cb_ref[:, cols], cols, tm))
        xcb = xc.astype(BF16)
        qk = _dot(xcb, wqk_ref[b])
        q_ref[:, cols] = qk[:, :MXU_DIM].astype(BF16)
        k_ref[:, cols] = qk[:, MXU_DIM:].astype(BF16)
        v_ref[:, cols] = _dot(xm.astype(BF16), wv_ref[b]).astype(BF16)
        xc_ref[:, cols] = xcb
        z_ref[:, cols] = _dot(hn, win_ref[:, inner + b * MXU_DIM:inner + (b + 1) * MXU_DIM]).astype(BF16)

    gc = (_dot(q_ref[...], wgq_ref[...]) + _dot(k_ref[...], wgk_ref[...]) + _dot(v_ref[...], wgv_ref[...])
          + bg_ref[...])
    lane = lax.broadcasted_iota(I32, gc.shape, 1)
    gc = jnp.where(lane < heads, gc, _log_sigmoid(gc))
    gcol_ref[...] = gc
    gr = gc.T
    for c in range(tm // lc):
        grow_ref[c] = gr[0:SUBLANES, c * lc:(c + 1) * lc]

    ti = lax.broadcasted_iota(I32, (lc, lc), 0)
    si = lax.broadcasted_iota(I32, (lc, lc), 1)
    causal = si <= ti

    def chunk_body(c, carry):
        r0 = pl.multiple_of(c * lc, lc)
        rows = pl.ds(r0, lc)
        gcol = gcol_ref[rows, :]
        grow = grow_ref[c]
        nb = n_ref[...].astype(BF16)
        for h in range(heads):
            cols = slice(h * hd, (h + 1) * hd)
            qc = q_ref[rows, cols]
            kc = k_ref[rows, cols]
            vc = v_ref[rows, cols]
            ig_c = gcol[:, h:h + 1]
            ig_r = grow[h:h + 1, :]
            lf_c = gcol[:, heads + h:heads + h + 1]
            lf_r = grow[heads + h:heads + h + 1, :]
            bcum_c = jnp.sum(jnp.where(causal, lf_r, 0.0), axis=1, keepdims=True)
            bcum_r = jnp.sum(jnp.where(ti <= si, lf_c, 0.0), axis=0, keepdims=True)
            m_st = m_ref[h:h + 1, 0:1]
            dm = jnp.where(causal, bcum_c - bcum_r + ig_r, -jnp.inf)
            inter = bcum_c + m_st
            m_t = jnp.maximum(inter, jnp.max(dm, axis=1, keepdims=True))
            dexp = jnp.exp(dm - m_t)
            sc = jnp.exp(inter - m_t)
            sco = (_dot_nt(qc, kc) * scale) * dexp
            cst = c_ref[h]
            num = _dot(sco.astype(BF16), vc) + sc * _dot(qc, cst.astype(BF16))
            qn = _dot_nt(qc, nb)[:, h:h + 1]
            den = jnp.sum(sco, axis=1, keepdims=True) + sc * qn
            hc = num * (1.0 / jnp.maximum(jnp.abs(den), jnp.exp(-m_t)))
            m_new = m_t[lc - 1:lc, :]
            b_last = bcum_c[lc - 1:lc, :]
            w_c = jnp.exp(b_last - bcum_c + ig_c - m_new) * scale
            w_r = jnp.exp(b_last - bcum_r + ig_r - m_new) * scale
            decay = jnp.exp(b_last + m_st - m_new)
            kw = kc * w_c.astype(BF16)
            c_ref[h] = decay * cst + _dot_tn(kw, vc)
            wk = _dot(jnp.broadcast_to(w_r, (SUBLANES, lc)).astype(BF16), kc)
            n_ref[h:h + 1, :] = decay * n_ref[h:h + 1, :] + wk[0:1, :]
            m_ref[h:h + 1, :] = jnp.broadcast_to(m_new, (1, LANES))
            hnrm = hc * lax.rsqrt(jnp.mean(hc * hc, axis=-1, keepdims=True) + RMS_EPS) * ng_ref[:, cols]
            zc = z_ref[rows, cols].astype(F32)
            xcc = xc_ref[rows, cols].astype(F32)
            y_ref[rows, cols] = (jax.nn.sigmoid(zc) * (hnrm + skip_ref[:, cols] * xcc)).astype(BF16)
        return carry

    lax.fori_loop(0, tm // lc, chunk_body, 0)
    o_ref[...] = x + _dot(y_ref[...], wout_ref[...])


def _mlstm_layer(x, g, w_in, conv_w, conv_b, w_q, w_k, w_v, w_gate, b_gate, norm_g, skip, w_out, heads,
                 layer):
    s, d = x.shape
    inner = w_out.shape[1]
    tm = min(TOKEN_TILE, s)
    ng = 2 * heads
    wqk = jnp.concatenate([_block_diag_tiles(w_q, MXU_DIM), _block_diag_tiles(w_k, MXU_DIM)],
                          axis=-1).astype(BF16)
    wv = _block_diag_tiles(w_v, MXU_DIM).astype(BF16)
    wgp = jnp.pad(w_gate, ((0, 0), (0, LANES - ng))).astype(BF16)
    wg_parts = [wgp[p * inner:(p + 1) * inner] for p in range(3)]
    bgp = jnp.pad(b_gate, (0, LANES - ng)).reshape(1, LANES)
    nqb = inner // MXU_DIM
    kern = functools.partial(_mlstm_kernel, tm=tm, inner=inner, heads=heads)
    in_specs = [_row_spec(tm, d), _const_spec((1, d)), _layer_spec(w_in, layer),
                _const_spec((CONV_W, inner)), _const_spec((1, inner)),
                _const_spec((nqb, MXU_DIM, 2 * MXU_DIM)), _const_spec((nqb, MXU_DIM, MXU_DIM))]
    in_specs += [_const_spec((inner, LANES))] * 3
    in_specs += [_const_spec((1, LANES)), _const_spec((1, inner)), _const_spec((1, inner)),
                 _layer_spec(w_out, layer)]
    hd = inner // heads
    scratch = [pltpu.VMEM((SUBLANES, inner), F32)]
    scratch += [pltpu.VMEM((tm, inner), BF16)] * 6
    scratch += [pltpu.VMEM((tm, LANES), F32), pltpu.VMEM((tm // ML_CHUNK, SUBLANES, ML_CHUNK), F32),
                pltpu.VMEM((heads, hd, hd), F32), pltpu.VMEM((SUBLANES, hd), F32),
                pltpu.VMEM((SUBLANES, LANES), F32)]
    return pl.pallas_call(
        kern,
        out_shape=jax.ShapeDtypeStruct((s, d), F32),
        grid=(s // tm,),
        in_specs=in_specs,
        out_specs=_row_spec(tm, d),
        scratch_shapes=scratch,
        compiler_params=_cparams(1, 56),
        name="mlstm_layer",
    )(x, g.reshape(1, d), w_in, conv_w, conv_b.reshape(1, inner), wqk, wv, *wg_parts, bgp,
      norm_g.reshape(1, inner), skip.reshape(1, inner), w_out)


def _s5_in_kernel(x_ref, g_ref, o_ref, slab_ref, *, tm):
    hn = _rms(x_ref[...], g_ref[...])
    nslab = hn.shape[1] // LANES
    for k in range(nslab):
        slab_ref[k] = hn[:, k * LANES:(k + 1) * LANES]
    for s in range(S5_CHUNK):
        for k in range(nslab):
            rows = slab_ref[k, pl.ds(s, tm // S5_CHUNK, stride=S5_CHUNK), :]
            o_ref[s, :, k * LANES:(k + 1) * LANES] = rows.astype(BF16)


def _s5_scan_kernel(xs_ref, t_ref, bc_ref, cc_ref, pre_ref, pim_ref, ys_ref,
                    sre_ref, sim_ref, cre_ref, cim_ref, *, cb, half):
    lc = S5_CHUNK

    @pl.when(pl.program_id(1) == 0)
    def _():
        cre_ref[...] = jnp.zeros_like(cre_ref)
        cim_ref[...] = jnp.zeros_like(cim_ref)

    u = jnp.concatenate([xs_ref[s] for s in range(lc)], axis=1)
    npair = (lc * LANES) // MXU_DIM
    ytiles = []
    for b in range(npair):
        acc = _dot(u[:, 0:MXU_DIM], t_ref[b])
        for a in range(1, b + 1):
            acc = acc + _dot(u[:, a * MXU_DIM:(a + 1) * MXU_DIM], t_ref[b - a])
        ytiles.append(acc)
    sinc = _dot(u, bc_ref[...])
    sre_ref[...] = sinc[:, :half]
    sim_ref[...] = sinc[:, half:]
    cin_re, cin_im = cre_ref[...], cim_ref[...]
    pr0, pi0 = pre_ref[0], pim_ref[0]
    sre_ref[0:1, :] = sre_ref[0:1, :] + (pr0 * cin_re - pi0 * cin_im)
    sim_ref[0:1, :] = sim_ref[0:1, :] + (pr0 * cin_im + pi0 * cin_re)
    xre, xim = sre_ref[...], sim_ref[...]
    row = lax.broadcasted_iota(I32, xre.shape, 0)
    d, kstep = 1, 0
    while d < cb:
        keep = row >= d
        re_sh = jnp.where(keep, pltpu.roll(xre, d, 0), 0.0)
        im_sh = jnp.where(keep, pltpu.roll(xim, d, 0), 0.0)
        pr, pi = pre_ref[kstep], pim_ref[kstep]
        xre, xim = xre + (pr * re_sh - pi * im_sh), xim + (pr * im_sh + pi * re_sh)
        d *= 2
        kstep += 1
    cre_ref[...] = xre[cb - 1:cb, :]
    cim_ref[...] = xim[cb - 1:cb, :]
    first = row == 0
    prev_re = jnp.where(first, cin_re, pltpu.roll(xre, 1, 0)).astype(BF16)
    prev_im = jnp.where(first, cin_im, pltpu.roll(xim, 1, 0)).astype(BF16)
    prev = jnp.concatenate([prev_re, prev_im], axis=1)
    for b in range(npair):
        yb = ytiles[b] + _dot(prev, cc_ref[:, b * MXU_DIM:(b + 1) * MXU_DIM])
        ys_ref[2 * b] = yb[:, :LANES]
        ys_ref[2 * b + 1] = yb[:, LANES:]


def _cmul(ar, ai, br, bi):
    return ar * br - ai * bi, ar * bi + ai * br


def _s5_operators(a_re, a_im, log_step, b_re, b_im, c_re, c_im, cb):
    lc = S5_CHUNK
    ng, np_, ni = b_re.shape
    gpv = LANES // ni
    nv = ng // gpv
    npair = lc // 2
    step = jnp.exp(log_step)[:, None]
    taus = jnp.arange(lc + 1, dtype=F32)[None, :, None]
    mag = jnp.exp((a_re * step)[:, None, :] * taus)
    ang = (a_im * step)[:, None, :] * taus
    pr, pi = mag * jnp.cos(ang), mag * jnp.sin(ang)
    lr, li = pr[:, 1], pi[:, 1]
    den = a_re * a_re + a_im * a_im
    fr, fi = _cmul(lr - 1.0, li, a_re / den, -a_im / den)
    bbr, bbi = _cmul(fr[..., None], fi[..., None], b_re, b_im)
    bbrt, bbit = bbr.transpose(0, 2, 1), bbi.transpose(0, 2, 1)
    mr, mi = _cmul(c_re[:, None], c_im[:, None], pr[:, :, None, :], pi[:, :, None, :])
    kk = jnp.sum(mr[:, :lc, None, :, :] * bbrt[:, None, :, None, :]
                 - mi[:, :lc, None, :, :] * bbit[:, None, :, None, :], axis=-1)
    grp16 = lambda r: (r // ni) % gpv
    zero_lag = jnp.zeros_like(kk[:, 0])
    kd = jnp.stack([jnp.stack([jnp.stack([kk[:, 2 * dd + t2 - s2] if 2 * dd + t2 - s2 >= 0 else zero_lag
                                          for t2 in range(2)], axis=-2)
                               for s2 in range(2)], axis=1)
                    for dd in range(npair)], axis=1)
    kd = kd.reshape(nv, gpv, npair, 2, ni, 2 * ni).transpose(0, 2, 3, 1, 4, 5)
    sp_t = jnp.tile(jnp.eye(2 * ni, dtype=F32).reshape(2 * ni, 2, 1, ni), (1, 1, gpv, 1)).reshape(2 * ni, -1)
    toep = _spread_masked(kd.reshape(-1, 2 * ni), sp_t, grp16, grp16).reshape(nv, npair, MXU_DIM, MXU_DIM)
    rtaus = (lc - 1) - taus[:, :lc]
    rmag = jnp.exp((a_re * step)[:, None, :] * rtaus)
    rang = (a_im * step)[:, None, :] * rtaus
    rev_r, rev_i = rmag * jnp.cos(rang), rmag * jnp.sin(rang)
    bcr, bci = _cmul(rev_r[:, :, None, :], rev_i[:, :, None, :], bbrt[:, None], bbit[:, None])
    bcx = jnp.concatenate([bcr, bci], axis=-1).reshape(nv, gpv, lc, ni, 2 * np_).transpose(0, 2, 1, 3, 4)
    sp_b = jnp.tile(jnp.eye(2 * np_, dtype=F32).reshape(2 * np_, 2, 1, np_), (1, 1, gpv, 1)).reshape(2 * np_, -1)
    bc = _spread_masked(bcx.reshape(-1, 2 * np_), sp_b, grp16, lambda c: (c // np_) % gpv)
    bc = bc.reshape(nv, lc * gpv * ni, 2 * gpv * np_)
    ccx = jnp.stack([mr[:, 1:lc + 1], -mi[:, 1:lc + 1]], axis=1)
    ccx = ccx.reshape(nv, gpv, 2, lc, ni, np_).transpose(0, 2, 1, 5, 3, 4)
    sp_c = jnp.tile(jnp.eye(lc * ni, dtype=F32).reshape(lc * ni, lc, 1, ni), (1, 1, gpv, 1)).reshape(lc * ni, -1)
    cc = _spread_masked(ccx.reshape(-1, lc * ni), sp_c, lambda r: (r // np_) % gpv, grp16)
    cc = cc.reshape(nv, 2 * gpv * np_, lc * gpv * ni)
    nsteps = max(1, int(math.ceil(math.log2(max(cb, 2)))))
    sq = jnp.stack([pr[:, lc].reshape(1, -1), pi[:, lc].reshape(1, -1)])
    scan = []
    for _ in range(nsteps):
        scan.append(sq)
        sq = jnp.stack(_cmul(sq[0], sq[1], sq[0], sq[1]))
    scan = jnp.stack(scan, axis=1)
    return toep, bc, cc, scan[0], scan[1]


def _s5_out_kernel(x_ref, ys_ref, g_ref, d_ref, wglu_ref, bglu_ref, o_ref, slab_ref, *, tm):
    x = x_ref[...]
    nslab = x.shape[1] // LANES
    for s in range(S5_CHUNK):
        for k in range(nslab):
            slab_ref[k, pl.ds(s, tm // S5_CHUNK, stride=S5_CHUNK), :] = ys_ref[s, :, k * LANES:(k + 1) * LANES]
    yt = jnp.concatenate([slab_ref[k] for k in range(nslab)], axis=1)
    hn = _rms(x, g_ref[...])
    y = _gelu_tanh(yt + d_ref[...] * hn)
    o_ref[...] = x + y * jax.nn.sigmoid(_dot(y.astype(BF16), wglu_ref[...]) + bglu_ref[...])


def _s5_layer(x, g, a_re, a_im, log_step, b_re, b_im, c_re, c_im, d_skip, w_glu, b_glu, layer):
    s, d = x.shape
    ngroups, pstate, gch = b_re.shape
    lc = S5_CHUNK
    nchunk = s // lc
    cb = min(S5_CHUNK_BLOCK, nchunk)
    gpv = LANES // gch
    nv = ngroups // gpv
    half = gpv * pstate
    toep, bc, cc, scan_r, scan_i = _s5_operators(a_re, a_im, log_step, b_re, b_im, c_re, c_im, cb)
    nsteps = scan_r.shape[0]
    tm = min(TOKEN_TILE, s)
    nslab = d // LANES
    step_major = pl.BlockSpec((lc, tm // lc, d), lambda i: (0, i, 0))
    xs = pl.pallas_call(
        functools.partial(_s5_in_kernel, tm=tm),
        out_shape=jax.ShapeDtypeStruct((lc, nchunk, d), BF16),
        grid=(s // tm,),
        in_specs=[_row_spec(tm, d), _const_spec((1, d))],
        out_specs=step_major,
        scratch_shapes=[pltpu.VMEM((nslab, tm, LANES), F32)],
        compiler_params=_cparams(1, 24),
        name="s5_in",
    )(x, g.reshape(1, d))
    wl = lc * LANES
    npair = lc // 2
    ys = pl.pallas_call(
        functools.partial(_s5_scan_kernel, cb=cb, half=half),
        out_shape=jax.ShapeDtypeStruct((lc, nchunk, d), F32),
        grid=(nv, nchunk // cb),
        in_specs=[pl.BlockSpec((lc, cb, LANES), lambda v, j: (0, j, v)),
                  pl.BlockSpec((None, npair, MXU_DIM, MXU_DIM), lambda v, j: (v, 0, 0, 0)),
                  pl.BlockSpec((None, wl, 2 * half), lambda v, j: (v, 0, 0)),
                  pl.BlockSpec((None, 2 * half, wl), lambda v, j: (v, 0, 0)),
                  pl.BlockSpec((nsteps, 1, half), lambda v, j: (0, 0, v)),
                  pl.BlockSpec((nsteps, 1, half), lambda v, j: (0, 0, v))],
        out_specs=pl.BlockSpec((lc, cb, LANES), lambda v, j: (0, j, v)),
        scratch_shapes=[pltpu.VMEM((cb, half), F32), pltpu.VMEM((cb, half), F32),
                        pltpu.VMEM((1, half), F32), pltpu.VMEM((1, half), F32)],
        compiler_params=_cparams(2, 52),
        name="s5_scan",
    )(xs, toep, bc, cc, scan_r, scan_i)
    return pl.pallas_call(
        functools.partial(_s5_out_kernel, tm=tm),
        out_shape=jax.ShapeDtypeStruct((s, d), F32),
        grid=(s // tm,),
        in_specs=[_row_spec(tm, d), step_major, _const_spec((1, d)), _const_spec((1, d)),
                  _layer_spec(w_glu, layer), _const_spec((1, d))],
        out_specs=_row_spec(tm, d),
        scratch_shapes=[pltpu.VMEM((nslab, tm, LANES), F32)],
        compiler_params=_cparams(1, 32),
        name="s5_out",
    )(x, ys, g.reshape(1, d), d_skip.reshape(1, d), w_glu, b_glu.reshape(1, d))


def _router_kernel(x_ref, g_ref, wr_ref, br_ref, hn_ref, wts_ref, idx_ref, cnt_ref, run_ref, *, tm, ne):
    @pl.when(pl.program_id(0) == 0)
    def _():
        run_ref[...] = jnp.zeros_like(run_ref)

    hn = _rms(x_ref[...], g_ref[...])
    for k in range(hn.shape[1] // LANES):
        hn_ref[pl.ds(k, tm, stride=SUBLANES), :] = hn[:, k * LANES:(k + 1) * LANES]
    wr = wr_ref[...]
    hn_hi, wr_hi = hn.astype(BF16), wr.astype(BF16)
    hn_lo, wr_lo = (hn - hn_hi.astype(F32)).astype(BF16), (wr - wr_hi.astype(F32)).astype(BF16)
    logits = (_dot(hn_hi, wr_hi) + _dot(hn_hi, wr_lo)) + (_dot(hn_lo, wr_hi) + _dot(hn_lo, wr_lo))
    logits = logits + br_ref[...]
    lane = lax.broadcasted_iota(I32, (tm, ne), 1).astype(F32)
    m1 = jnp.max(logits, axis=1, keepdims=True)
    i1 = jnp.min(jnp.where(logits == m1, lane, float(ne)), axis=1, keepdims=True)
    sel1 = lane == i1
    rest = jnp.where(sel1, -jnp.inf, logits)
    m2 = jnp.max(rest, axis=1, keepdims=True)
    i2 = jnp.min(jnp.where(rest == m2, lane, float(ne)), axis=1, keepdims=True)
    sel2 = lane == i2
    e = jnp.exp(m2 - m1)
    w1 = 1.0 / (1.0 + e)
    w2 = e / (1.0 + e)
    sel = jnp.logical_or(sel1, sel2)
    ti = lax.broadcasted_iota(I32, (tm, tm), 0)
    si = lax.broadcasted_iota(I32, (tm, tm), 1)
    earlier = jnp.where(si < ti, 1.0, 0.0).astype(BF16)
    excl = _dot(earlier, jnp.where(sel, 1.0, 0.0).astype(BF16)) + run_ref[...]
    r1 = jnp.sum(jnp.where(sel1, excl, 0.0), axis=1, keepdims=True)
    r2 = jnp.sum(jnp.where(sel2, excl, 0.0), axis=1, keepdims=True)
    total = run_ref[...] + jnp.sum(jnp.where(sel, 1.0, 0.0), axis=0, keepdims=True)
    run_ref[...] = total
    cnt_ref[...] = total
    wts_ref[...] = jnp.where(lane == 0.0, w1, jnp.where(lane == 1.0, w2, 0.0))
    lane_w = lax.broadcasted_iota(I32, (tm, LANES), 1)
    meta = jnp.zeros((tm, LANES), F32)
    for c, val in enumerate([i1, i2, r1, r2]):
        meta = jnp.where(lane_w == c, val, meta)
    idx_ref[...] = meta.T[0:SUBLANES, :].astype(I32)


def _token_rows(t):
    return pl.ds(pl.multiple_of(t * SUBLANES, SUBLANES), SUBLANES)


def _untile_tokens(ref, n, nslab, lead=()):
    return jnp.concatenate([ref[lead + (pl.ds(k, n, stride=SUBLANES), slice(None))] for k in range(nslab)], axis=1)


def _dispatch_kernel(ends_ref, pos_ref, hn_ref, xs_ref, zero_ref, sem, zsem, *, tg, tr, ne):
    @pl.when(pl.program_id(0) == 0)
    def _():
        zero_ref[...] = jnp.zeros_like(zero_ref)
        last_tile = xs_ref.shape[0] // SUBLANES - tr
        for e in range(ne):
            for r0 in (jnp.maximum(ends_ref[e] - tr, 0), jnp.minimum(ends_ref[ne - 1] + e * tr, last_tile)):
                dst = xs_ref.at[pl.ds(pl.multiple_of(r0 * SUBLANES, tr * SUBLANES), tr * SUBLANES), :]
                cp = pltpu.make_async_copy(zero_ref, dst, zsem)
                cp.start()
                cp.wait()

    def row_copy(t, dst_row):
        return pltpu.make_async_copy(hn_ref.at[_token_rows(t), :], xs_ref.at[_token_rows(dst_row), :], sem)

    def issue(t, carry):
        for k in range(TOP_K):
            row_copy(t, pos_ref[k, t]).start()
        return carry

    lax.fori_loop(0, tg, issue, 0, unroll=8)

    def drain(t, carry):
        for k in range(TOP_K):
            row_copy(t, 0).wait()
        return carry

    lax.fori_loop(0, tg, drain, 0, unroll=8)


def _expert_kernel(te_ref, nv_ref, xs_ref, w1_ref, w3_ref, w2_ref, ys_ref, acc_ref, *, tr):
    i = pl.program_id(0)
    j = pl.program_id(1)
    nslab = acc_ref.shape[1] // LANES

    @pl.when(i < nv_ref[0])
    def _():
        xb = _untile_tokens(xs_ref, tr, nslab).astype(BF16)
        h = (_silu(_dot(xb, w1_ref[...])) * _dot(xb, w3_ref[...])).astype(BF16)
        part = _dot(h, w2_ref[...])

        @pl.when(j == 0)
        def _():
            acc_ref[...] = part

        @pl.when(j != 0)
        def _():
            y = acc_ref[...] + part
            for k in range(nslab):
                ys_ref[pl.ds(k, tr, stride=SUBLANES), :] = y[:, k * LANES:(k + 1) * LANES]

    @pl.when(i >= nv_ref[0])
    def _():
        ys_ref[...] = jnp.zeros_like(ys_ref)


def _combine_kernel(pos_ref, x_ref, wts_ref, gf_ref, ys_ref, o_ref, buf_ref, sem, *, tg, final_norm):
    def row_copy(t, k, src_row):
        return pltpu.make_async_copy(ys_ref.at[_token_rows(src_row), :], buf_ref.at[k, _token_rows(t), :], sem)

    def issue(t, carry):
        for k in range(TOP_K):
            row_copy(t, k, pos_ref[k, t]).start()
        return carry

    lax.fori_loop(0, tg, issue, 0, unroll=8)

    def drain(t, carry):
        for k in range(TOP_K):
            row_copy(t, k, 0).wait()
        return carry

    lax.fori_loop(0, tg, drain, 0, unroll=8)

    x = x_ref[...]
    nslab = x.shape[1] // LANES
    wts = wts_ref[...]
    out = x + (wts[:, 0:1] * _untile_tokens(buf_ref, tg, nslab, (0,))
               + wts[:, 1:2] * _untile_tokens(buf_ref, tg, nslab, (1,)))
    if final_norm:
        out = _rms(out, gf_ref[...])
    o_ref[...] = out


def _moe_layer(x, g, router, router_b, w1, w3, w2, layer, g_final, final_norm):
    s, d = x.shape
    ne = router.shape[1]
    f = w1.shape[3]
    tm = min(TOKEN_TILE, s)
    tg = min(GATHER_TILE, s)
    tr = GROUP_TILE
    sub = SUBLANES
    hn, wts, idx, cnt = pl.pallas_call(
        functools.partial(_router_kernel, tm=tm, ne=ne),
        out_shape=(jax.ShapeDtypeStruct((s * sub, LANES), F32), jax.ShapeDtypeStruct((s, ne), F32),
                   jax.ShapeDtypeStruct((sub, s), I32), jax.ShapeDtypeStruct((1, ne), F32)),
        grid=(s // tm,),
        in_specs=[_row_spec(tm, d), _const_spec((1, d)), _const_spec((d, ne)), _const_spec((1, ne))],
        out_specs=(_row_spec(tm * sub, LANES), _row_spec(tm, ne), pl.BlockSpec((sub, tm), lambda i: (0, i)),
                   pl.BlockSpec((1, ne), lambda i: (0, 0))),
        scratch_shapes=[pltpu.VMEM((1, ne), F32)],
        compiler_params=_cparams(1, 24),
        name="moe_router",
    )(x, g.reshape(1, d), router, router_b.reshape(1, ne))

    counts = cnt[0].astype(I32)
    padded = ((counts + tr - 1) // tr) * tr
    ends = jnp.cumsum(padded)
    starts = ends - padded
    n_tiles = (TOP_K * s) // tr + ne
    rows = n_tiles * tr
    n_valid = ends[-1] // tr
    tile_row = jnp.minimum(jnp.arange(n_tiles, dtype=I32), n_valid - 1) * tr
    tile_expert = jnp.minimum(jnp.sum((ends[None, :] <= tile_row[:, None]).astype(I32), axis=1), ne - 1)

    expert_start = jnp.sum(jnp.where(idx[0:TOP_K, :, None] == jnp.arange(ne, dtype=I32), starts, 0), axis=-1)
    pos = expert_start + idx[TOP_K:2 * TOP_K]

    pos_spec = pl.BlockSpec((TOP_K, tg), lambda i, *_: (0, i), memory_space=pltpu.SMEM)
    any_spec = pl.BlockSpec(memory_space=pl.ANY)
    xs = pl.pallas_call(
        functools.partial(_dispatch_kernel, tg=tg, tr=tr, ne=ne),
        out_shape=jax.ShapeDtypeStruct((rows * sub, LANES), F32),
        grid_spec=pltpu.PrefetchScalarGridSpec(
            num_scalar_prefetch=1,
            grid=(s // tg,),
            in_specs=[pos_spec, pl.BlockSpec((tg * sub, LANES), lambda i, *_: (i, 0))],
            out_specs=any_spec,
            scratch_shapes=[pltpu.VMEM((tr * sub, LANES), F32), pltpu.SemaphoreType.DMA,
                            pltpu.SemaphoreType.DMA],
        ),
        compiler_params=_cparams(1, 16),
        name="moe_dispatch",
    )(ends, pos, hn)

    tf = f // 2
    row_tile = lambda i, j, te, nv: (jnp.maximum(jnp.minimum(i, nv[0] - 1), 0), 0)
    ys = pl.pallas_call(
        functools.partial(_expert_kernel, tr=tr),
        out_shape=jax.ShapeDtypeStruct((rows * sub, LANES), F32),
        grid_spec=pltpu.PrefetchScalarGridSpec(
            num_scalar_prefetch=2,
            grid=(n_tiles, f // tf),
            in_specs=[pl.BlockSpec((tr * sub, LANES), row_tile),
                      pl.BlockSpec((None, None, d, tf), lambda i, j, te, nv: (layer, te[i], 0, j)),
                      pl.BlockSpec((None, None, d, tf), lambda i, j, te, nv: (layer, te[i], 0, j)),
                      pl.BlockSpec((None, None, tf, d), lambda i, j, te, nv: (layer, te[i], j, 0))],
            out_specs=pl.BlockSpec((tr * sub, LANES), lambda i, j, te, nv: (i, 0)),
            scratch_shapes=[pltpu.VMEM((tr, d), F32)],
        ),
        compiler_params=_cparams(2, 48),
        name="moe_experts",
    )(tile_expert, n_valid.reshape(1), xs, w1, w3, w2)

    return pl.pallas_call(
        functools.partial(_combine_kernel, tg=tg, final_norm=final_norm),
        out_shape=jax.ShapeDtypeStruct((s, d), F32),
        grid=(s // tg,),
        in_specs=[pl.BlockSpec((TOP_K, tg), lambda i: (0, i), memory_space=pltpu.SMEM), _row_spec(tg, d),
                  _row_spec(tg, ne), _const_spec((1, d)), any_spec],
        out_specs=_row_spec(tg, d),
        scratch_shapes=[pltpu.VMEM((TOP_K, tg * sub, LANES), F32), pltpu.SemaphoreType.DMA],
        compiler_params=_cparams(1, 24),
        name="moe_combine",
    )(pos, x, wts, g_final.reshape(1, d), ys)


def _norm_kernel(x_ref, g_ref, o_ref):
    o_ref[...] = _rms(x_ref[...], g_ref[...])


def _final_norm(x, g):
    s, d = x.shape
    tm = min(TOKEN_TILE, s)
    return pl.pallas_call(
        _norm_kernel,
        out_shape=jax.ShapeDtypeStruct((s, d), F32),
        grid=(s // tm,),
        in_specs=[_row_spec(tm, d), _const_spec((1, d))],
        out_specs=_row_spec(tm, d),
        compiler_params=_cparams(1, 16),
        name="final_norm",
    )(x, g.reshape(1, d))


def kernel(x, norm_mix, norm_ffn, norm_final, a_w_in, a_conv_w, a_conv_b, a_w_r, a_b_r, a_w_i, a_b_i, a_lam, a_w_out, b_w_in, b_conv_w, b_conv_b, b_w_q, b_w_k, b_w_v, b_w_gate, b_b_gate, b_norm, b_skip, b_w_out, c_a_re, c_a_im, c_log_step, c_b_re, c_b_im, c_c_re, c_c_im, c_d, c_w_glu, c_b_glu, f_w1, f_w3, f_w2, e_router, e_router_b, e_w1, e_w3, e_w2):
    bsz, seq, d = x.shape
    depth = norm_mix.shape[0]
    heads = b_b_gate.shape[1] // 2
    a_w_in, a_w_out, b_w_in, b_w_out, c_w_glu = (w.astype(BF16) for w in (a_w_in, a_w_out, b_w_in, b_w_out,
                                                                            c_w_glu))
    f_w1, f_w3, f_w2, e_w1, e_w3, e_w2 = (w.astype(BF16) for w in (f_w1, f_w3, f_w2, e_w1, e_w3, e_w2))
    outs = []
    for bi in range(bsz):
        h = x[bi]
        for i in range(depth):
            kind, j = i % 3, i // 3
            if kind == 0:
                h = _rglru_layer(h, norm_mix[i], a_w_in, a_conv_w[j], a_conv_b[j], a_w_r[j], a_b_r[j],
                                 a_w_i[j], a_b_i[j], a_lam[j], a_w_out, j)
            elif kind == 1:
                h = _mlstm_layer(h, norm_mix[i], b_w_in, b_conv_w[j], b_conv_b[j], b_w_q[j], b_w_k[j],
                                 b_w_v[j], b_w_gate[j], b_b_gate[j], b_norm[j], b_skip[j], b_w_out, heads, j)
            else:
                h = _s5_layer(h, norm_mix[i], c_a_re[j], c_a_im[j], c_log_step[j], c_b_re[j], c_b_im[j],
                              c_c_re[j], c_c_im[j], c_d[j], c_w_glu, c_b_glu[j], j)
            fidx = i // 2
            last = i == depth - 1
            if i % 2 == 0:
                h = _ffn_dense(h, norm_ffn[i], f_w1, f_w3, f_w2, fidx)
                if last:
                    h = _final_norm(h, norm_final)
            else:
                h = _moe_layer(h, norm_ffn[i], e_router[fidx], e_router_b[fidx], e_w1, e_w3, e_w2, fidx,
                               norm_final, last)
        outs.append(h)
    return jnp.stack(outs)
```

```python
import functools
import math

import jax
import jax.numpy as jnp
from jax import lax
from jax.experimental import pallas as pl
from jax.experimental.pallas import tpu as pltpu

F32 = jnp.float32
BF16 = jnp.bfloat16
I32 = jnp.int32

RMS_EPS = 1e-6
CONV_W = 4
RG_C = 8.0
ML_CHUNK = 128
TOP_K = 2

V7X_VMEM_BYTES = 64 * 1024 * 1024
SUBLANES = 8
LANES = 128
MXU_DIM = 256

TOKEN_TILE = 512
GATHER_TILE = 256
GROUP_TILE = 512
S5_CHUNK = 16
S5_CHUNK_BLOCK = 512


def _cparams(n_axes, vmem_mib):
    return pltpu.CompilerParams(
        dimension_semantics=("arbitrary",) * n_axes,
        vmem_limit_bytes=min(vmem_mib * 1024 * 1024, V7X_VMEM_BYTES - 6 * 1024 * 1024),
    )


def _const_spec(shape):
    nd = len(shape)
    return pl.BlockSpec(shape, lambda *_: (0,) * nd, pipeline_mode=pl.Buffered(1))


def _layer_spec(stacked, layer):
    nd = stacked.ndim
    return pl.BlockSpec((None,) + tuple(stacked.shape[1:]), lambda *_: (layer,) + (0,) * (nd - 1),
                        pipeline_mode=pl.Buffered(1))


def _row_spec(tm, d):
    return pl.BlockSpec((tm, d), lambda i: (i, 0))


def _rms(x, g):
    return x * lax.rsqrt(jnp.mean(x * x, axis=-1, keepdims=True) + RMS_EPS) * g


def _dot(a, b):
    return jnp.dot(a, b, preferred_element_type=F32)


def _dot_nt(a, b):
    return lax.dot_general(a, b, (((1,), (1,)), ((), ())), preferred_element_type=F32)


def _dot_tn(a, b):
    return lax.dot_general(a, b, (((0,), (0,)), ((), ())), preferred_element_type=F32)


def _silu(x):
    return x * jax.nn.sigmoid(x)


def _gelu_tanh(x):
    return 0.5 * x * (1.0 + jnp.tanh(math.sqrt(2.0 / math.pi) * (x + 0.044715 * (x * x * x))))


def _log_sigmoid(x):
    return jnp.minimum(x, 0.0) - jnp.log1p(jnp.exp(-jnp.abs(x)))


def _causal_conv(tail_ref, xin, cw, cb, cols, tm):
    prev = tail_ref[:, cols]
    sub = lax.broadcasted_iota(I32, prev.shape, 0)
    out = cb
    for d in (3, 2, 1):
        sh = pltpu.roll(xin, d, 0)
        head = jnp.where(sub < d, pltpu.roll(prev, d, 0), sh[0:SUBLANES])
        out = out + cw[3 - d:4 - d] * jnp.concatenate([head, sh[SUBLANES:]], axis=0)
    out = out + cw[3:4] * xin
    tail_ref[:, cols] = xin[tm - SUBLANES:tm, :]
    return out


def _spread_masked(x, spread, row_group, col_group):
    y = jnp.dot(x.astype(BF16), spread.astype(BF16), preferred_element_type=F32)
    keep = row_group(jnp.arange(x.shape[0]))[:, None] == col_group(jnp.arange(spread.shape[1]))[None, :]
    return jnp.where(keep, y, 0.0).astype(BF16)


def _block_diag_tiles(w, tile):
    nb, k, _ = w.shape
    per = tile // k
    spread = jnp.tile(jnp.eye(k, dtype=F32), (1, per))
    tiles = _spread_masked(w.reshape(nb * k, k), spread, lambda r: (r // k) % per, lambda c: c // k)
    return tiles.reshape(nb // per, tile, tile)


def _ffn_dense_kernel(x_ref, g_ref, w1_ref, w3_ref, w2_ref, o_ref, *, ff_chunk):
    x = x_ref[...]
    hn = _rms(x, g_ref[...]).astype(BF16)
    acc = x
    for c in range(w1_ref.shape[1] // ff_chunk):
        sl = slice(c * ff_chunk, (c + 1) * ff_chunk)
        h = (_silu(_dot(hn, w1_ref[:, sl])) * _dot(hn, w3_ref[:, sl])).astype(BF16)
        acc = acc + _dot(h, w2_ref[sl, :])
    o_ref[...] = acc


def _ffn_dense(x, g, w1, w3, w2, layer):
    s, d = x.shape
    tm = min(TOKEN_TILE, s)
    return pl.pallas_call(
        functools.partial(_ffn_dense_kernel, ff_chunk=MXU_DIM),
        out_shape=jax.ShapeDtypeStruct((s, d), F32),
        grid=(s // tm,),
        in_specs=[_row_spec(tm, d), _const_spec((1, d)), _layer_spec(w1, layer), _layer_spec(w3, layer),
                  _layer_spec(w2, layer)],
        out_specs=_row_spec(tm, d),
        compiler_params=_cparams(1, 48),
        name="ffn_dense",
    )(x, g.reshape(1, d), w1, w3, w2)


def _linear_scan_rows(a, b, h0):
    tm, c = a.shape
    ngrp = tm // SUBLANES
    a3 = a.reshape(ngrp, SUBLANES, c)
    b3 = b.reshape(ngrp, SUBLANES, c)
    sub = lax.broadcasted_iota(I32, (1, SUBLANES, c), 1)
    for d in (1, 2, 4):
        keep = sub >= d
        a_sh = jnp.where(keep, pltpu.roll(a3, d, 1), 1.0)
        b_sh = jnp.where(keep, pltpu.roll(b3, d, 1), 0.0)
        b3 = a3 * b_sh + b3
        a3 = a3 * a_sh
    carry = h0
    rows = []
    for j in range(ngrp):
        hj = b3[j] + a3[j] * carry
        rows.append(hj)
        carry = hj[SUBLANES - 1:SUBLANES, :]
    return jnp.concatenate(rows, axis=0), carry


def _rglru_kernel(x_ref, g_ref, win_ref, cw_ref, cb_ref, wg_ref, bg_ref, nsp_ref, wout_ref, o_ref,
                  tail_ref, h_ref, *, tm, width):
    bd = MXU_DIM

    @pl.when(pl.program_id(0) == 0)
    def _():
        tail_ref[...] = jnp.zeros_like(tail_ref)
        h_ref[...] = jnp.zeros_like(h_ref)

    x = x_ref[...]
    hn = _rms(x, g_ref[...]).astype(BF16)
    ys = []
    for n in range(width // bd):
        cols = slice(n * bd, (n + 1) * bd)
        gx = _dot(hn, win_ref[:, cols])
        xr = _dot(hn, win_ref[:, width + n * bd:width + (n + 1) * bd])
        xr = _causal_conv(tail_ref, xr, cw_ref[:, cols], cb_ref[:, cols], cols, tm)
        gates = _dot(xr.astype(BF16), wg_ref[n]) + bg_ref[n]
        r = jax.nn.sigmoid(gates[:, :bd])
        ig = jax.nn.sigmoid(gates[:, bd:])
        a = jnp.exp(r * nsp_ref[:, cols])
        v = 1.0 - a * a
        b = jnp.where(v > 0.0, v * lax.rsqrt(v), 0.0) * (ig * xr)
        h, last = _linear_scan_rows(a, b, h_ref[:, cols])
        h_ref[:, cols] = last
        ys.append((_gelu_tanh(gx) * h).astype(BF16))
    y = jnp.concatenate(ys, axis=1)
    o_ref[...] = x + _dot(y, wout_ref[...])


def _rglru_layer(x, g, w_in, conv_w, conv_b, w_r, b_r, w_i, b_i, lam, w_out, layer):
    s, d = x.shape
    width = w_out.shape[1]
    bd = MXU_DIM
    nblk = width // bd
    tm = min(TOKEN_TILE, s)
    wg = jnp.concatenate([_block_diag_tiles(w_r, bd), _block_diag_tiles(w_i, bd)], axis=-1).astype(BF16)
    bg = jnp.concatenate([b_r.reshape(nblk, 1, bd), b_i.reshape(nblk, 1, bd)], axis=-1)
    nsp = (-RG_C * jax.nn.softplus(-lam)).reshape(1, width)
    kern = functools.partial(_rglru_kernel, tm=tm, width=width)
    return pl.pallas_call(
        kern,
        out_shape=jax.ShapeDtypeStruct((s, d), F32),
        grid=(s // tm,),
        in_specs=[_row_spec(tm, d), _const_spec((1, d)), _layer_spec(w_in, layer),
                  _const_spec((CONV_W, width)), _const_spec((1, width)),
                  _const_spec((nblk, bd, 2 * bd)), _const_spec((nblk, 1, 2 * bd)),
                  _const_spec((1, width)), _layer_spec(w_out, layer)],
        out_specs=_row_spec(tm, d),
        scratch_shapes=[pltpu.VMEM((SUBLANES, width), F32), pltpu.VMEM((1, width), F32)],
        compiler_params=_cparams(1, 40),
        name="rglru_layer",
    )(x, g.reshape(1, d), w_in, conv_w, conv_b.reshape(1, width), wg, bg, nsp, w_out)


def _mlstm_kernel(x_ref, g_ref, win_ref, cw_ref, cb_ref, wqk_ref, wv_ref, wgq_ref, wgk_ref, wgv_ref,
                  bg_ref, ng_ref, skip_ref, wout_ref, o_ref,
                  tail_ref, q_ref, k_ref, v_ref, xc_ref, z_ref, y_ref, gcol_ref, grow_ref,
                  c_ref, n_ref, m_ref, *, tm, inner, heads):
    hd = inner // heads
    lc = ML_CHUNK
    nqb = inner // MXU_DIM
    scale = hd ** -0.5

    @pl.when(pl.program_id(0) == 0)
    def _():
        tail_ref[...] = jnp.zeros_like(tail_ref)
        c_ref[...] = jnp.zeros_like(c_ref)
        n_ref[...] = jnp.zeros_like(n_ref)
        m_ref[...] = jnp.zeros_like(m_ref)

    x = x_ref[...]
    hn = _rms(x, g_ref[...]).astype(BF16)

    for b in range(nqb):
        cols = slice(b * MXU_DIM, (b + 1) * MXU_DIM)
        xm = _dot(hn, win_ref[:, cols])
        xc = _silu(_causal_conv(tail_ref, xm, cw_ref[:, cols], cb_ref[:, cols], cols, tm))
        xcb = xc.astype(BF16)
        qk = _dot(xcb, wqk_ref[b])
        q_ref[:, cols] = qk[:, :MXU_DIM].astype(BF16)
        k_ref[:, cols] = qk[:, MXU_DIM:].astype(BF16)
        v_ref[:, cols] = _dot(xm.astype(BF16), wv_ref[b]).astype(BF16)
        xc_ref[:, cols] = xcb
        z_ref[:, cols] = _dot(hn, win_ref[:, inner + b * MXU_DIM:inner + (b + 1) * MXU_DIM]).astype(BF16)

    gc = (_dot(q_ref[...], wgq_ref[...]) + _dot(k_ref[...], wgk_ref[...]) + _dot(v_ref[...], wgv_ref[...])
          + bg_ref[...])
    lane = lax.broadcasted_iota(I32, gc.shape, 1)
    gc = jnp.where(lane < heads, gc, _log_sigmoid(gc))
    gcol_ref[...] = gc
    gr = gc.T
    for c in range(tm // lc):
        grow_ref[c] = gr[0:SUBLANES, c * lc:(c + 1) * lc]

    ti = lax.broadcasted_iota(I32, (lc, lc), 0)
    si = lax.broadcasted_iota(I32, (lc, lc), 1)
    causal = si <= ti

    def chunk_body(c, carry):
        r0 = pl.multiple_of(c * lc, lc)
        rows = pl.ds(r0, lc)
        gcol = gcol_ref[rows, :]
        grow = grow_ref[c]
        nb = n_ref[...].astype(BF16)
        for h in range(heads):
            cols = slice(h * hd, (h + 1) * hd)
            qc = q_ref[rows, cols]
            kc = k_ref[rows, cols]
            vc = v_ref[rows, cols]
            ig_c = gcol[:, h:h + 1]
            ig_r = grow[h:h + 1, :]
            lf_c = gcol[:, heads + h:heads + h + 1]
            lf_r = grow[heads + h:heads + h + 1, :]
            bcum_c = jnp.sum(jnp.where(causal, lf_r, 0.0), axis=1, keepdims=True)
            bcum_r = jnp.sum(jnp.where(ti <= si, lf_c, 0.0), axis=0, keepdims=True)
            m_st = m_ref[h:h + 1, 0:1]
            dm = jnp.where(causal, bcum_c - bcum_r + ig_r, -jnp.inf)
            inter = bcum_c + m_st
            m_t = jnp.maximum(inter, jnp.max(dm, axis=1, keepdims=True))
            dexp = jnp.exp(dm - m_t)
            sc = jnp.exp(inter - m_t)
            sco = (_dot_nt(qc, kc) * scale) * dexp
            cst = c_ref[h]
            num = _dot(sco.astype(BF16), vc) + sc * _dot(qc, cst.astype(BF16))
            qn = _dot_nt(qc, nb)[:, h:h + 1]
            den = jnp.sum(sco, axis=1, keepdims=True) + sc * qn
            hc = num * (1.0 / jnp.maximum(jnp.abs(den), jnp.exp(-m_t)))
            m_new = m_t[lc - 1:lc, :]
            b_last = bcum_c[lc - 1:lc, :]
            w_c = jnp.exp(b_last - bcum_c + ig_c - m_new) * scale
            w_r = jnp.exp(b_last - bcum_r + ig_r - m_new) * scale
            decay = jnp.exp(b_last + m_st - m_new)
            kw = kc * w_c.astype(BF16)
            c_ref[h] = decay * cst + _dot_tn(kw, vc)
            wk = _dot(jnp.broadcast_to(w_r, (SUBLANES, lc)).astype(BF16), kc)
            n_ref[h:h + 1, :] = decay * n_ref[h:h + 1, :] + wk[0:1, :]
            m_ref[h:h + 1, :] = jnp.broadcast_to(m_new, (1, LANES))
            hnrm = hc * lax.rsqrt(jnp.mean(hc * hc, axis=-1, keepdims=True) + RMS_EPS) * ng_ref[:, cols]
            zc = z_ref[rows, cols].astype(F32)
            xcc = xc_ref[rows, cols].astype(F32)
            y_ref[rows, cols] = (jax.nn.sigmoid(zc) * (hnrm + skip_ref[:, cols] * xcc)).astype(BF16)
        return carry

    lax.fori_loop(0, tm // lc, chunk_body, 0, unroll=True)
    o_ref[...] = x + _dot(y_ref[...], wout_ref[...])


def _mlstm_layer(x, g, w_in, conv_w, conv_b, w_q, w_k, w_v, w_gate, b_gate, norm_g, skip, w_out, heads,
                 layer):
    s, d = x.shape
    inner = w_out.shape[1]
    tm = min(TOKEN_TILE, s)
    ng = 2 * heads
    wqk = jnp.concatenate([_block_diag_tiles(w_q, MXU_DIM), _block_diag_tiles(w_k, MXU_DIM)],
                          axis=-1).astype(BF16)
    wv = _block_diag_tiles(w_v, MXU_DIM).astype(BF16)
    wgp = jnp.pad(w_gate, ((0, 0), (0, LANES - ng))).astype(BF16)
    wg_parts = [wgp[p * inner:(p + 1) * inner] for p in range(3)]
    bgp = jnp.pad(b_gate, (0, LANES - ng)).reshape(1, LANES)
    nqb = inner // MXU_DIM
    kern = functools.partial(_mlstm_kernel, tm=tm, inner=inner, heads=heads)
    in_specs = [_row_spec(tm, d), _const_spec((1, d)), _layer_spec(w_in, layer),
                _const_spec((CONV_W, inner)), _const_spec((1, inner)),
                _const_spec((nqb, MXU_DIM, 2 * MXU_DIM)), _const_spec((nqb, MXU_DIM, MXU_DIM))]
    in_specs += [_const_spec((inner, LANES))] * 3
    in_specs += [_const_spec((1, LANES)), _const_spec((1, inner)), _const_spec((1, inner)),
                 _layer_spec(w_out, layer)]
    hd = inner // heads
    scratch = [pltpu.VMEM((SUBLANES, inner), F32)]
    scratch += [pltpu.VMEM((tm, inner), BF16)] * 6
    scratch += [pltpu.VMEM((tm, LANES), F32), pltpu.VMEM((tm // ML_CHUNK, SUBLANES, ML_CHUNK), F32),
                pltpu.VMEM((heads, hd, hd), F32), pltpu.VMEM((SUBLANES, hd), F32),
                pltpu.VMEM((SUBLANES, LANES), F32)]
    return pl.pallas_call(
        kern,
        out_shape=jax.ShapeDtypeStruct((s, d), F32),
        grid=(s // tm,),
        in_specs=in_specs,
        out_specs=_row_spec(tm, d),
        scratch_shapes=scratch,
        compiler_params=_cparams(1, 56),
        name="mlstm_layer",
    )(x, g.reshape(1, d), w_in, conv_w, conv_b.reshape(1, inner), wqk, wv, *wg_parts, bgp,
      norm_g.reshape(1, inner), skip.reshape(1, inner), w_out)


def _s5_in_kernel(x_ref, g_ref, o_ref, slab_ref, *, tm):
    hn = _rms(x_ref[...], g_ref[...])
    nslab = hn.shape[1] // LANES
    for k in range(nslab):
        slab_ref[k] = hn[:, k * LANES:(k + 1) * LANES]
    for s in range(S5_CHUNK):
        for k in range(nslab):
            rows = slab_ref[k, pl.ds(s, tm // S5_CHUNK, stride=S5_CHUNK), :]
            o_ref[s, :, k * LANES:(k + 1) * LANES] = rows.astype(BF16)


def _s5_scan_kernel(xs_ref, t_ref, bc_ref, cc_ref, pre_ref, pim_ref, ys_ref,
                    sre_ref, sim_ref, cre_ref, cim_ref, *, cb, half):
    lc = S5_CHUNK

    @pl.when(pl.program_id(1) == 0)
    def _():
        cre_ref[...] = jnp.zeros_like(cre_ref)
        cim_ref[...] = jnp.zeros_like(cim_ref)

    u = jnp.concatenate([xs_ref[s] for s in range(lc)], axis=1)
    npair = (lc * LANES) // MXU_DIM
    ytiles = []
    for b in range(npair):
        acc = _dot(u[:, 0:MXU_DIM], t_ref[b])
        for a in range(1, b + 1):
            acc = acc + _dot(u[:, a * MXU_DIM:(a + 1) * MXU_DIM], t_ref[b - a])
        ytiles.append(acc)
    sinc = _dot(u, bc_ref[...])
    sre_ref[...] = sinc[:, :half]
    sim_ref[...] = sinc[:, half:]
    cin_re, cin_im = cre_ref[...], cim_ref[...]
    pr0, pi0 = pre_ref[0], pim_ref[0]
    sre_ref[0:1, :] = sre_ref[0:1, :] + (pr0 * cin_re - pi0 * cin_im)
    sim_ref[0:1, :] = sim_ref[0:1, :] + (pr0 * cin_im + pi0 * cin_re)
    xre, xim = sre_ref[...], sim_ref[...]
    row = lax.broadcasted_iota(I32, xre.shape, 0)
    d, kstep = 1, 0
    while d < cb:
        keep = row >= d
        re_sh = jnp.where(keep, pltpu.roll(xre, d, 0), 0.0)
        im_sh = jnp.where(keep, pltpu.roll(xim, d, 0), 0.0)
        pr, pi = pre_ref[kstep], pim_ref[kstep]
        xre, xim = xre + (pr * re_sh - pi * im_sh), xim + (pr * im_sh + pi * re_sh)
        d *= 2
        kstep += 1
    cre_ref[...] = xre[cb - 1:cb, :]
    cim_ref[...] = xim[cb - 1:cb, :]
    first = row == 0
    prev_re = jnp.where(first, cin_re, pltpu.roll(xre, 1, 0)).astype(BF16)
    prev_im = jnp.where(first, cin_im, pltpu.roll(xim, 1, 0)).astype(BF16)
    prev = jnp.concatenate([prev_re, prev_im], axis=1)
    for b in range(npair):
        yb = ytiles[b] + _dot(prev, cc_ref[:, b * MXU_DIM:(b + 1) * MXU_DIM])
        ys_ref[2 * b] = yb[:, :LANES]
        ys_ref[2 * b + 1] = yb[:, LANES:]


def _cmul(ar, ai, br, bi):
    return ar * br - ai * bi, ar * bi + ai * br


def _s5_operators(a_re, a_im, log_step, b_re, b_im, c_re, c_im, cb):
    lc = S5_CHUNK
    ng, np_, ni = b_re.shape
    gpv = LANES // ni
    nv = ng // gpv
    npair = lc // 2
    step = jnp.exp(log_step)[:, None]
    taus = jnp.arange(lc + 1, dtype=F32)[None, :, None]
    mag = jnp.exp((a_re * step)[:, None, :] * taus)
    ang = (a_im * step)[:, None, :] * taus
    pr, pi = mag * jnp.cos(ang), mag * jnp.sin(ang)
    lr, li = pr[:, 1], pi[:, 1]
    den = a_re * a_re + a_im * a_im
    fr, fi = _cmul(lr - 1.0, li, a_re / den, -a_im / den)
    bbr, bbi = _cmul(fr[..., None], fi[..., None], b_re, b_im)
    bbrt, bbit = bbr.transpose(0, 2, 1), bbi.transpose(0, 2, 1)
    mr, mi = _cmul(c_re[:, None], c_im[:, None], pr[:, :, None, :], pi[:, :, None, :])
    kk = jnp.sum(mr[:, :lc, None, :, :] * bbrt[:, None, :, None, :]
                 - mi[:, :lc, None, :, :] * bbit[:, None, :, None, :], axis=-1)
    grp16 = lambda r: (r // ni) % gpv
    zero_lag = jnp.zeros_like(kk[:, 0])
    kd = jnp.stack([jnp.stack([jnp.stack([kk[:, 2 * dd + t2 - s2] if 2 * dd + t2 - s2 >= 0 else zero_lag
                                          for t2 in range(2)], axis=-2)
                               for s2 in range(2)], axis=1)
                    for dd in range(npair)], axis=1)
    kd = kd.reshape(nv, gpv, npair, 2, ni, 2 * ni).transpose(0, 2, 3, 1, 4, 5)
    sp_t = jnp.tile(jnp.eye(2 * ni, dtype=F32).reshape(2 * ni, 2, 1, ni), (1, 1, gpv, 1)).reshape(2 * ni, -1)
    toep = _spread_masked(kd.reshape(-1, 2 * ni), sp_t, grp16, grp16).reshape(nv, npair, MXU_DIM, MXU_DIM)
    rtaus = (lc - 1) - taus[:, :lc]
    rmag = jnp.exp((a_re * step)[:, None, :] * rtaus)
    rang = (a_im * step)[:, None, :] * rtaus
    rev_r, rev_i = rmag * jnp.cos(rang), rmag * jnp.sin(rang)
    bcr, bci = _cmul(rev_r[:, :, None, :], rev_i[:, :, None, :], bbrt[:, None], bbit[:, None])
    bcx = jnp.concatenate([bcr, bci], axis=-1).reshape(nv, gpv, lc, ni, 2 * np_).transpose(0, 2, 1, 3, 4)
    sp_b = jnp.tile(jnp.eye(2 * np_, dtype=F32).reshape(2 * np_, 2, 1, np_), (1, 1, gpv, 1)).reshape(2 * np_, -1)
    bc = _spread_masked(bcx.reshape(-1, 2 * np_), sp_b, grp16, lambda c: (c // np_) % gpv)
    bc = bc.reshape(nv, lc * gpv * ni, 2 * gpv * np_)
    ccx = jnp.stack([mr[:, 1:lc + 1], -mi[:, 1:lc + 1]], axis=1)
    ccx = ccx.reshape(nv, gpv, 2, lc, ni, np_).transpose(0, 2, 1, 5, 3, 4)
    sp_c = jnp.tile(jnp.eye(lc * ni, dtype=F32).reshape(lc * ni, lc, 1, ni), (1, 1, gpv, 1)).reshape(lc * ni, -1)
    cc = _spread_masked(ccx.reshape(-1, lc * ni), sp_c, lambda r: (r // np_) % gpv, grp16)
    cc = cc.reshape(nv, 2 * gpv * np_, lc * gpv * ni)
    nsteps = max(1, int(math.ceil(math.log2(max(cb, 2)))))
    sq = jnp.stack([pr[:, lc].reshape(1, -1), pi[:, lc].reshape(1, -1)])
    scan = []
    for _ in range(nsteps):
        scan.append(sq)
        sq = jnp.stack(_cmul(sq[0], sq[1], sq[0], sq[1]))
    scan = jnp.stack(scan, axis=1)
    return toep, bc, cc, scan[0], scan[1]


def _s5_out_kernel(x_ref, ys_ref, g_ref, d_ref, wglu_ref, bglu_ref, o_ref, slab_ref, *, tm):
    x = x_ref[...]
    nslab = x.shape[1] // LANES
    for s in range(S5_CHUNK):
        for k in range(nslab):
            slab_ref[k, pl.ds(s, tm // S5_CHUNK, stride=S5_CHUNK), :] = ys_ref[s, :, k * LANES:(k + 1) * LANES]
    yt = jnp.concatenate([slab_ref[k] for k in range(nslab)], axis=1)
    hn = _rms(x, g_ref[...])
    y = _gelu_tanh(yt + d_ref[...] * hn)
    o_ref[...] = x + y * jax.nn.sigmoid(_dot(y.astype(BF16), wglu_ref[...]) + bglu_ref[...])


def _s5_layer(x, g, a_re, a_im, log_step, b_re, b_im, c_re, c_im, d_skip, w_glu, b_glu, layer):
    s, d = x.shape
    ngroups, pstate, gch = b_re.shape
    lc = S5_CHUNK
    nchunk = s // lc
    cb = min(S5_CHUNK_BLOCK, nchunk)
    gpv = LANES // gch
    nv = ngroups // gpv
    half = gpv * pstate
    toep, bc, cc, scan_r, scan_i = _s5_operators(a_re, a_im, log_step, b_re, b_im, c_re, c_im, cb)
    nsteps = scan_r.shape[0]
    tm = min(TOKEN_TILE, s)
    nslab = d // LANES
    step_major = pl.BlockSpec((lc, tm // lc, d), lambda i: (0, i, 0))
    xs = pl.pallas_call(
        functools.partial(_s5_in_kernel, tm=tm),
        out_shape=jax.ShapeDtypeStruct((lc, nchunk, d), BF16),
        grid=(s // tm,),
        in_specs=[_row_spec(tm, d), _const_spec((1, d))],
        out_specs=step_major,
        scratch_shapes=[pltpu.VMEM((nslab, tm, LANES), F32)],
        compiler_params=_cparams(1, 24),
        name="s5_in",
    )(x, g.reshape(1, d))
    wl = lc * LANES
    npair = lc // 2
    ys = pl.pallas_call(
        functools.partial(_s5_scan_kernel, cb=cb, half=half),
        out_shape=jax.ShapeDtypeStruct((lc, nchunk, d), F32),
        grid=(nv, nchunk // cb),
        in_specs=[pl.BlockSpec((lc, cb, LANES), lambda v, j: (0, j, v)),
                  pl.BlockSpec((None, npair, MXU_DIM, MXU_DIM), lambda v, j: (v, 0, 0, 0)),
                  pl.BlockSpec((None, wl, 2 * half), lambda v, j: (v, 0, 0)),
                  pl.BlockSpec((None, 2 * half, wl), lambda v, j: (v, 0, 0)),
                  pl.BlockSpec((nsteps, 1, half), lambda v, j: (0, 0, v)),
                  pl.BlockSpec((nsteps, 1, half), lambda v, j: (0, 0, v))],
        out_specs=pl.BlockSpec((lc, cb, LANES), lambda v, j: (0, j, v)),
        scratch_shapes=[pltpu.VMEM((cb, half), F32), pltpu.VMEM((cb, half), F32),
                        pltpu.VMEM((1, half), F32), pltpu.VMEM((1, half), F32)],
        compiler_params=_cparams(2, 52),
        name="s5_scan",
    )(xs, toep, bc, cc, scan_r, scan_i)
    return pl.pallas_call(
        functools.partial(_s5_out_kernel, tm=tm),
        out_shape=jax.ShapeDtypeStruct((s, d), F32),
        grid=(s // tm,),
        in_specs=[_row_spec(tm, d), step_major, _const_spec((1, d)), _const_spec((1, d)),
                  _layer_spec(w_glu, layer), _const_spec((1, d))],
        out_specs=_row_spec(tm, d),
        scratch_shapes=[pltpu.VMEM((nslab, tm, LANES), F32)],
        compiler_params=_cparams(1, 32),
        name="s5_out",
    )(x, ys, g.reshape(1, d), d_skip.reshape(1, d), w_glu, b_glu.reshape(1, d))


def _router_kernel(x_ref, g_ref, wr_ref, br_ref, hn_ref, wts_ref, idx_ref, cnt_ref, run_ref, *, tm, ne):
    @pl.when(pl.program_id(0) == 0)
    def _():
        run_ref[...] = jnp.zeros_like(run_ref)

    hn = _rms(x_ref[...], g_ref[...])
    for k in range(hn.shape[1] // LANES):
        hn_ref[pl.ds(k, tm, stride=SUBLANES), :] = hn[:, k * LANES:(k + 1) * LANES]
    wr = wr_ref[...]
    hn_hi, wr_hi = hn.astype(BF16), wr.astype(BF16)
    hn_lo, wr_lo = (hn - hn_hi.astype(F32)).astype(BF16), (wr - wr_hi.astype(F32)).astype(BF16)
    logits = (_dot(hn_hi, wr_hi) + _dot(hn_hi, wr_lo)) + (_dot(hn_lo, wr_hi) + _dot(hn_lo, wr_lo))
    logits = logits + br_ref[...]
    lane = lax.broadcasted_iota(I32, (tm, ne), 1).astype(F32)
    m1 = jnp.max(logits, axis=1, keepdims=True)
    i1 = jnp.min(jnp.where(logits == m1, lane, float(ne)), axis=1, keepdims=True)
    sel1 = lane == i1
    rest = jnp.where(sel1, -jnp.inf, logits)
    m2 = jnp.max(rest, axis=1, keepdims=True)
    i2 = jnp.min(jnp.where(rest == m2, lane, float(ne)), axis=1, keepdims=True)
    sel2 = lane == i2
    e = jnp.exp(m2 - m1)
    w1 = 1.0 / (1.0 + e)
    w2 = e / (1.0 + e)
    sel = jnp.logical_or(sel1, sel2)
    ti = lax.broadcasted_iota(I32, (tm, tm), 0)
    si = lax.broadcasted_iota(I32, (tm, tm), 1)
    earlier = jnp.where(si < ti, 1.0, 0.0).astype(BF16)
    excl = _dot(earlier, jnp.where(sel, 1.0, 0.0).astype(BF16)) + run_ref[...]
    r1 = jnp.sum(jnp.where(sel1, excl, 0.0), axis=1, keepdims=True)
    r2 = jnp.sum(jnp.where(sel2, excl, 0.0), axis=1, keepdims=True)
    total = run_ref[...] + jnp.sum(jnp.where(sel, 1.0, 0.0), axis=0, keepdims=True)
    run_ref[...] = total
    cnt_ref[...] = total
    wts_ref[...] = jnp.where(lane == 0.0, w1, jnp.where(lane == 1.0, w2, 0.0))
    lane_w = lax.broadcasted_iota(I32, (tm, LANES), 1)
    meta = jnp.zeros((tm, LANES), F32)
    for c, val in enumerate([i1, i2, r1, r2]):
        meta = jnp.where(lane_w == c, val, meta)
    idx_ref[...] = meta.T[0:SUBLANES, :].astype(I32)


def _token_rows(t):
    return pl.ds(pl.multiple_of(t * SUBLANES, SUBLANES), SUBLANES)


def _untile_tokens(ref, n, nslab, lead=()):
    return jnp.concatenate([ref[lead + (pl.ds(k, n, stride=SUBLANES), slice(None))] for k in range(nslab)], axis=1)


def _dispatch_kernel(ends_ref, pos_ref, hn_ref, xs_ref, zero_ref, sem, zsem, *, tg, tr, ne):
    @pl.when(pl.program_id(0) == 0)
    def _():
        zero_ref[...] = jnp.zeros_like(zero_ref)
        last_tile = xs_ref.shape[0] // SUBLANES - tr
        for e in range(ne):
            for r0 in (jnp.maximum(ends_ref[e] - tr, 0), jnp.minimum(ends_ref[ne - 1] + e * tr, last_tile)):
                dst = xs_ref.at[pl.ds(pl.multiple_of(r0 * SUBLANES, tr * SUBLANES), tr * SUBLANES), :]
                cp = pltpu.make_async_copy(zero_ref, dst, zsem)
                cp.start()
                cp.wait()

    def row_copy(t, dst_row):
        return pltpu.make_async_copy(hn_ref.at[_token_rows(t), :], xs_ref.at[_token_rows(dst_row), :], sem)

    def issue(t, carry):
        for k in range(TOP_K):
            row_copy(t, pos_ref[k, t]).start(priority=k)
        return carry

    lax.fori_loop(0, tg, issue, 0, unroll=8)

    def drain(t, carry):
        for k in range(TOP_K):
            row_copy(t, 0).wait()
        return carry

    lax.fori_loop(0, tg, drain, 0, unroll=8)


def _expert_kernel(te_ref, nv_ref, xs_ref, w1_ref, w3_ref, w2_ref, ys_ref, *, tr, ff_chunk):
    i = pl.program_id(0)
    nslab = w2_ref.shape[1] // LANES

    @pl.when(i < nv_ref[0])
    def _():
        xb = _untile_tokens(xs_ref, tr, nslab).astype(BF16)
        y = None
        for c in range(w1_ref.shape[1] // ff_chunk):
            sl = slice(c * ff_chunk, (c + 1) * ff_chunk)
            h = (_silu(_dot(xb, w1_ref[:, sl])) * _dot(xb, w3_ref[:, sl])).astype(BF16)
            part = _dot(h, w2_ref[sl, :])
            y = part if y is None else y + part
        for k in range(nslab):
            ys_ref[pl.ds(k, tr, stride=SUBLANES), :] = y[:, k * LANES:(k + 1) * LANES]

    @pl.when(i >= nv_ref[0])
    def _():
        ys_ref[...] = jnp.zeros_like(ys_ref)


def _combine_kernel(pos_ref, x_ref, wts_ref, gf_ref, ys_ref, o_ref, buf_ref, sem, *, tg, final_norm):
    def row_copy(t, k, src_row):
        return pltpu.make_async_copy(ys_ref.at[_token_rows(src_row), :], buf_ref.at[k, _token_rows(t), :], sem)

    def issue(t, carry):
        for k in range(TOP_K):
            row_copy(t, k, pos_ref[k, t]).start(priority=k)
        return carry

    lax.fori_loop(0, tg, issue, 0, unroll=8)

    def drain(t, carry):
        for k in range(TOP_K):
            row_copy(t, k, 0).wait()
        return carry

    lax.fori_loop(0, tg, drain, 0, unroll=8)

    x = x_ref[...]
    nslab = x.shape[1] // LANES
    wts = wts_ref[...]
    out = x + (wts[:, 0:1] * _untile_tokens(buf_ref, tg, nslab, (0,))
               + wts[:, 1:2] * _untile_tokens(buf_ref, tg, nslab, (1,)))
    if final_norm:
        out = _rms(out, gf_ref[...])
    o_ref[...] = out


def _moe_layer(x, g, router, router_b, w1, w3, w2, layer, g_final, final_norm):
    s, d = x.shape
    ne = router.shape[1]
    f = w1.shape[3]
    tm = min(TOKEN_TILE, s)
    tg = min(GATHER_TILE, s)
    tr = GROUP_TILE
    sub = SUBLANES
    hn, wts, idx, cnt = pl.pallas_call(
        functools.partial(_router_kernel, tm=tm, ne=ne),
        out_shape=(jax.ShapeDtypeStruct((s * sub, LANES), F32), jax.ShapeDtypeStruct((s, ne), F32),
                   jax.ShapeDtypeStruct((sub, s), I32), jax.ShapeDtypeStruct((1, ne), F32)),
        grid=(s // tm,),
        in_specs=[_row_spec(tm, d), _const_spec((1, d)), _const_spec((d, ne)), _const_spec((1, ne))],
        out_specs=(_row_spec(tm * sub, LANES), _row_spec(tm, ne), pl.BlockSpec((sub, tm), lambda i: (0, i)),
                   pl.BlockSpec((1, ne), lambda i: (0, 0))),
        scratch_shapes=[pltpu.VMEM((1, ne), F32)],
        compiler_params=_cparams(1, 24),
        name="moe_router",
    )(x, g.reshape(1, d), router, router_b.reshape(1, ne))

    counts = cnt[0].astype(I32)
    padded = ((counts + tr - 1) // tr) * tr
    ends = jnp.cumsum(padded)
    starts = ends - padded
    n_tiles = (TOP_K * s) // tr + ne
    rows = n_tiles * tr
    n_valid = ends[-1] // tr
    tile_row = jnp.minimum(jnp.arange(n_tiles, dtype=I32), n_valid - 1) * tr
    tile_expert = jnp.minimum(jnp.sum((ends[None, :] <= tile_row[:, None]).astype(I32), axis=1), ne - 1)

    expert_start = jnp.sum(jnp.where(idx[0:TOP_K, :, None] == jnp.arange(ne, dtype=I32), starts, 0), axis=-1)
    pos = expert_start + idx[TOP_K:2 * TOP_K]

    pos_spec = pl.BlockSpec((TOP_K, tg), lambda i, *_: (0, i), memory_space=pltpu.SMEM)
    any_spec = pl.BlockSpec(memory_space=pl.ANY)
    xs = pl.pallas_call(
        functools.partial(_dispatch_kernel, tg=tg, tr=tr, ne=ne),
        out_shape=jax.ShapeDtypeStruct((rows * sub, LANES), F32),
        grid_spec=pltpu.PrefetchScalarGridSpec(
            num_scalar_prefetch=1,
            grid=(s // tg,),
            in_specs=[pos_spec, pl.BlockSpec((tg * sub, LANES), lambda i, *_: (i, 0))],
            out_specs=any_spec,
            scratch_shapes=[pltpu.VMEM((tr * sub, LANES), F32), pltpu.SemaphoreType.DMA,
                            pltpu.SemaphoreType.DMA],
        ),
        compiler_params=_cparams(1, 16),
        name="moe_dispatch",
    )(ends, pos, hn)

    row_tile = lambda i, te, nv: (jnp.maximum(jnp.minimum(i, nv[0] - 1), 0), 0)
    expert_w = lambda i, te, nv: (layer, te[i], 0, 0)
    ys = pl.pallas_call(
        functools.partial(_expert_kernel, tr=tr, ff_chunk=MXU_DIM),
        out_shape=jax.ShapeDtypeStruct((rows * sub, LANES), F32),
        grid_spec=pltpu.PrefetchScalarGridSpec(
            num_scalar_prefetch=2,
            grid=(n_tiles,),
            in_specs=[pl.BlockSpec((tr * sub, LANES), row_tile),
                      pl.BlockSpec((None, None, d, f), expert_w),
                      pl.BlockSpec((None, None, d, f), expert_w),
                      pl.BlockSpec((None, None, f, d), expert_w)],
            out_specs=pl.BlockSpec((tr * sub, LANES), lambda i, te, nv: (i, 0)),
        ),
        compiler_params=_cparams(1, 54),
        name="moe_experts",
    )(tile_expert, n_valid.reshape(1), xs, w1, w3, w2)

    return pl.pallas_call(
        functools.partial(_combine_kernel, tg=tg, final_norm=final_norm),
        out_shape=jax.ShapeDtypeStruct((s, d), F32),
        grid=(s // tg,),
        in_specs=[pl.BlockSpec((TOP_K, tg), lambda i: (0, i), memory_space=pltpu.SMEM), _row_spec(tg, d),
                  _row_spec(tg, ne), _const_spec((1, d)), any_spec],
        out_specs=_row_spec(tg, d),
        scratch_shapes=[pltpu.VMEM((TOP_K, tg * sub, LANES), F32), pltpu.SemaphoreType.DMA],
        compiler_params=_cparams(1, 24),
        name="moe_combine",
    )(pos, x, wts, g_final.reshape(1, d), ys)


def _norm_kernel(x_ref, g_ref, o_ref):
    o_ref[...] = _rms(x_ref[...], g_ref[...])


def _final_norm(x, g):
    s, d = x.shape
    tm = min(TOKEN_TILE, s)
    return pl.pallas_call(
        _norm_kernel,
        out_shape=jax.ShapeDtypeStruct((s, d), F32),
        grid=(s // tm,),
        in_specs=[_row_spec(tm, d), _const_spec((1, d))],
        out_specs=_row_spec(tm, d),
        compiler_params=_cparams(1, 16),
        name="final_norm",
    )(x, g.reshape(1, d))


def kernel(x, norm_mix, norm_ffn, norm_final, a_w_in, a_conv_w, a_conv_b, a_w_r, a_b_r, a_w_i, a_b_i, a_lam, a_w_out, b_w_in, b_conv_w, b_conv_b, b_w_q, b_w_k, b_w_v, b_w_gate, b_b_gate, b_norm, b_skip, b_w_out, c_a_re, c_a_im, c_log_step, c_b_re, c_b_im, c_c_re, c_c_im, c_d, c_w_glu, c_b_glu, f_w1, f_w3, f_w2, e_router, e_router_b, e_w1, e_w3, e_w2):
    bsz, seq, d = x.shape
    depth = norm_mix.shape[0]
    heads = b_b_gate.shape[1] // 2
    a_w_in, a_w_out, b_w_in, b_w_out, c_w_glu = (w.astype(BF16) for w in (a_w_in, a_w_out, b_w_in, b_w_out,
                                                                            c_w_glu))
    f_w1, f_w3, f_w2, e_w1, e_w3, e_w2 = (w.astype(BF16) for w in (f_w1, f_w3, f_w2, e_w1, e_w3, e_w2))
    outs = []
    for bi in range(bsz):
        h = x[bi]
        for i in range(depth):
            kind, j = i % 3, i // 3
            if kind == 0:
                h = _rglru_layer(h, norm_mix[i], a_w_in, a_conv_w[j], a_conv_b[j], a_w_r[j], a_b_r[j],
                                 a_w_i[j], a_b_i[j], a_lam[j], a_w_out, j)
            elif kind == 1:
                h = _mlstm_layer(h, norm_mix[i], b_w_in, b_conv_w[j], b_conv_b[j], b_w_q[j], b_w_k[j],
                                 b_w_v[j], b_w_gate[j], b_b_gate[j], b_norm[j], b_skip[j], b_w_out, heads, j)
            else:
                h = _s5_layer(h, norm_mix[i], c_a_re[j], c_a_im[j], c_log_step[j], c_b_re[j], c_b_im[j],
                              c_c_re[j], c_c_im[j], c_d[j], c_w_glu, c_b_glu[j], j)
            fidx = i // 2
            last = i == depth - 1
            if i % 2 == 0:
                h = _ffn_dense(h, norm_ffn[i], f_w1, f_w3, f_w2, fidx)
                if last:
                    h = _final_norm(h, norm_final)
            else:
                h = _moe_layer(h, norm_ffn[i], e_router[fidx], e_router_b[fidx], e_w1, e_w3, e_w2, fidx,
                               norm_final, last)
        outs.append(h)
    return jnp.stack(outs)
```

```python
import functools
import math

import jax
import jax.numpy as jnp
import numpy as np
from jax import lax
from jax.experimental import pallas as pl
from jax.experimental.pallas import tpu as pltpu

F32 = jnp.float32
BF16 = jnp.bfloat16
I32 = jnp.int32

RMS_EPS = 1e-6
CONV_W = 4
RG_C = 8.0
ML_CHUNK = 128
TOP_K = 2

V7X_VMEM_BYTES = 64 * 1024 * 1024
SUBLANES = 8
LANES = 128
MXU_DIM = 256

TOKEN_TILE = 512
GATHER_TILE = 256
GROUP_TILE = 512
S5_CHUNK = 16
S5_CHUNK_BLOCK = 512


def _cparams(n_axes, vmem_mib):
    return pltpu.CompilerParams(
        dimension_semantics=("arbitrary",) * n_axes,
        vmem_limit_bytes=min(vmem_mib * 1024 * 1024, V7X_VMEM_BYTES - 6 * 1024 * 1024),
    )


def _const_spec(shape):
    nd = len(shape)
    return pl.BlockSpec(shape, lambda *_: (0,) * nd, pipeline_mode=pl.Buffered(1))


def _layer_spec(stacked, layer):
    nd = stacked.ndim
    return pl.BlockSpec((None,) + tuple(stacked.shape[1:]), lambda *_: (layer,) + (0,) * (nd - 1),
                        pipeline_mode=pl.Buffered(1))


def _row_spec(tm, d):
    return pl.BlockSpec((tm, d), lambda i: (i, 0))


def _rms(x, g):
    return x * lax.rsqrt(jnp.mean(x * x, axis=-1, keepdims=True) + RMS_EPS) * g


def _dot(a, b):
    return jnp.dot(a, b, preferred_element_type=F32)


def _dot_nt(a, b):
    return lax.dot_general(a, b, (((1,), (1,)), ((), ())), preferred_element_type=F32)


def _dot_tn(a, b):
    return lax.dot_general(a, b, (((0,), (0,)), ((), ())), preferred_element_type=F32)


def _silu(x):
    return x * jax.nn.sigmoid(x)


def _gelu_tanh(x):
    return 0.5 * x * (1.0 + jnp.tanh(math.sqrt(2.0 / math.pi) * (x + 0.044715 * (x * x * x))))


def _log_sigmoid(x):
    return jnp.minimum(x, 0.0) - jnp.log1p(jnp.exp(-jnp.abs(x)))


def _causal_conv(tail_ref, xin, cw, cb, cols, tm):
    prev = tail_ref[:, cols]
    sub = lax.broadcasted_iota(I32, prev.shape, 0)
    out = cb
    for d in (3, 2, 1):
        sh = pltpu.roll(xin, d, 0)
        head = jnp.where(sub < d, pltpu.roll(prev, d, 0), sh[0:SUBLANES])
        out = out + cw[3 - d:4 - d] * jnp.concatenate([head, sh[SUBLANES:]], axis=0)
    out = out + cw[3:4] * xin
    tail_ref[:, cols] = xin[tm - SUBLANES:tm, :]
    return out


def _spread_masked(x, spread, row_group, col_group):
    y = jnp.dot(x.astype(BF16), spread.astype(BF16), preferred_element_type=F32)
    keep = row_group(jnp.arange(x.shape[0]))[:, None] == col_group(jnp.arange(spread.shape[1]))[None, :]
    return jnp.where(keep, y, 0.0).astype(BF16)


def _block_diag_tiles(w, tile):
    nb, k, _ = w.shape
    per = tile // k
    spread = jnp.tile(jnp.eye(k, dtype=F32), (1, per))
    tiles = _spread_masked(w.reshape(nb * k, k), spread, lambda r: (r // k) % per, lambda c: c // k)
    return tiles.reshape(nb // per, tile, tile)


def _ffn_dense_kernel(x_ref, g_ref, w1_ref, w3_ref, w2_ref, o_ref, *, ff_chunk):
    x = x_ref[...]
    hn = _rms(x, g_ref[...]).astype(BF16)
    acc = x
    for c in range(w1_ref.shape[1] // ff_chunk):
        sl = slice(c * ff_chunk, (c + 1) * ff_chunk)
        h = (_silu(_dot(hn, w1_ref[:, sl])) * _dot(hn, w3_ref[:, sl])).astype(BF16)
        acc = acc + _dot(h, w2_ref[sl, :])
    o_ref[...] = acc


def _ffn_dense(x, g, w1, w3, w2, layer):
    s, d = x.shape
    tm = min(TOKEN_TILE, s)
    return pl.pallas_call(
        functools.partial(_ffn_dense_kernel, ff_chunk=MXU_DIM),
        out_shape=jax.ShapeDtypeStruct((s, d), F32),
        grid=(s // tm,),
        in_specs=[_row_spec(tm, d), _const_spec((1, d)), _layer_spec(w1, layer), _layer_spec(w3, layer),
                  _layer_spec(w2, layer)],
        out_specs=_row_spec(tm, d),
        compiler_params=_cparams(1, 48),
        name="ffn_dense",
    )(x, g.reshape(1, d), w1, w3, w2)


def _linear_scan_rows(a, b, h0):
    tm, c = a.shape
    ngrp = tm // SUBLANES
    a3 = a.reshape(ngrp, SUBLANES, c)
    b3 = b.reshape(ngrp, SUBLANES, c)
    sub = lax.broadcasted_iota(I32, (1, SUBLANES, c), 1)
    for d in (1, 2, 4):
        keep = sub >= d
        a_sh = jnp.where(keep, pltpu.roll(a3, d, 1), 1.0)
        b_sh = jnp.where(keep, pltpu.roll(b3, d, 1), 0.0)
        b3 = a3 * b_sh + b3
        a3 = a3 * a_sh
    carry = h0
    rows = []
    for j in range(ngrp):
        hj = b3[j] + a3[j] * carry
        rows.append(hj)
        carry = hj[SUBLANES - 1:SUBLANES, :]
    return jnp.concatenate(rows, axis=0), carry


def _rglru_kernel(x_ref, g_ref, win_ref, cw_ref, cb_ref, wg_ref, bg_ref, nsp_ref, wout_ref, o_ref,
                  tail_ref, h_ref, *, tm, width):
    bd = MXU_DIM

    @pl.when(pl.program_id(0) == 0)
    def _():
        tail_ref[...] = jnp.zeros_like(tail_ref)
        h_ref[...] = jnp.zeros_like(h_ref)

    x = x_ref[...]
    hn = _rms(x, g_ref[...]).astype(BF16)
    ys = []
    for n in range(width // bd):
        cols = slice(n * bd, (n + 1) * bd)
        gx = _dot(hn, win_ref[:, cols])
        xr = _dot(hn, win_ref[:, width + n * bd:width + (n + 1) * bd])
        xr = _causal_conv(tail_ref, xr, cw_ref[:, cols], cb_ref[:, cols], cols, tm)
        gates = _dot(xr.astype(BF16), wg_ref[n]) + bg_ref[n]
        r = jax.nn.sigmoid(gates[:, :bd])
        ig = jax.nn.sigmoid(gates[:, bd:])
        a = jnp.exp(r * nsp_ref[:, cols])
        v = 1.0 - a * a
        b = jnp.where(v > 0.0, v * lax.rsqrt(v), 0.0) * (ig * xr)
        h, last = _linear_scan_rows(a, b, h_ref[:, cols])
        h_ref[:, cols] = last
        ys.append((_gelu_tanh(gx) * h).astype(BF16))
    y = jnp.concatenate(ys, axis=1)
    o_ref[...] = x + _dot(y, wout_ref[...])


def _rglru_layer(x, g, w_in, conv_w, conv_b, w_r, b_r, w_i, b_i, lam, w_out, layer):
    s, d = x.shape
    width = w_out.shape[1]
    bd = MXU_DIM
    nblk = width // bd
    tm = min(TOKEN_TILE, s)
    wg = jnp.concatenate([_block_diag_tiles(w_r, bd), _block_diag_tiles(w_i, bd)], axis=-1).astype(BF16)
    bg = jnp.concatenate([b_r.reshape(nblk, 1, bd), b_i.reshape(nblk, 1, bd)], axis=-1)
    nsp = (-RG_C * jax.nn.softplus(-lam)).reshape(1, width)
    kern = functools.partial(_rglru_kernel, tm=tm, width=width)
    return pl.pallas_call(
        kern,
        out_shape=jax.ShapeDtypeStruct((s, d), F32),
        grid=(s // tm,),
        in_specs=[_row_spec(tm, d), _const_spec((1, d)), _layer_spec(w_in, layer),
                  _const_spec((CONV_W, width)), _const_spec((1, width)),
                  _const_spec((nblk, bd, 2 * bd)), _const_spec((nblk, 1, 2 * bd)),
                  _const_spec((1, width)), _layer_spec(w_out, layer)],
        out_specs=_row_spec(tm, d),
        scratch_shapes=[pltpu.VMEM((SUBLANES, width), F32), pltpu.VMEM((1, width), F32)],
        compiler_params=_cparams(1, 40),
        name="rglru_layer",
    )(x, g.reshape(1, d), w_in, conv_w, conv_b.reshape(1, width), wg, bg, nsp, w_out)


def _mlstm_kernel(x_ref, g_ref, win_ref, cw_ref, cb_ref, wqk_ref, wv_ref, wgq_ref, wgk_ref, wgv_ref,
                  bg_ref, ng_ref, skip_ref, wout_ref, o_ref,
                  tail_ref, q_ref, k_ref, v_ref, xc_ref, z_ref, y_ref, gcol_ref, grow_ref,
                  c_ref, n_ref, m_ref, *, tm, inner, heads):
    hd = inner // heads
    lc = ML_CHUNK
    nqb = inner // MXU_DIM
    scale = hd ** -0.5

    @pl.when(pl.program_id(0) == 0)
    def _():
        tail_ref[...] = jnp.zeros_like(tail_ref)
        c_ref[...] = jnp.zeros_like(c_ref)
        n_ref[...] = jnp.zeros_like(n_ref)
        m_ref[...] = jnp.zeros_like(m_ref)

    x = x_ref[...]
    hn = _rms(x, g_ref[...]).astype(BF16)

    for b in range(nqb):
        cols = slice(b * MXU_DIM, (b + 1) * MXU_DIM)
        xm = _dot(hn, win_ref[:, cols])
        xc = _silu(_causal_conv(tail_ref, xm, cw_ref[:, cols], cb_ref[:, cols], cols, tm))
        xcb = xc.astype(BF16)
        qk = _dot(xcb, wqk_ref[b])
        q_ref[:, cols] = qk[:, :MXU_DIM].astype(BF16)
        k_ref[:, cols] = qk[:, MXU_DIM:].astype(BF16)
        v_ref[:, cols] = _dot(xm.astype(BF16), wv_ref[b]).astype(BF16)
        xc_ref[:, cols] = xcb
        z_ref[:, cols] = _dot(hn, win_ref[:, inner + b * MXU_DIM:inner + (b + 1) * MXU_DIM]).astype(BF16)

    gc = (_dot(q_ref[...], wgq_ref[...]) + _dot(k_ref[...], wgk_ref[...]) + _dot(v_ref[...], wgv_ref[...])
          + bg_ref[...])
    lane = lax.broadcasted_iota(I32, gc.shape, 1)
    gc = jnp.where(lane < heads, gc, _log_sigmoid(gc))
    gcol_ref[...] = gc
    gr = gc.T
    for c in range(tm // lc):
        grow_ref[c] = gr[0:SUBLANES, c * lc:(c + 1) * lc]

    ti = lax.broadcasted_iota(I32, (lc, lc), 0)
    si = lax.broadcasted_iota(I32, (lc, lc), 1)
    causal = si <= ti

    def chunk_body(c, carry):
        r0 = pl.multiple_of(c * lc, lc)
        rows = pl.ds(r0, lc)
        gcol = gcol_ref[rows, :]
        grow = grow_ref[c]
        nb = n_ref[...].astype(BF16)
        for h in range(heads):
            cols = slice(h * hd, (h + 1) * hd)
            qc = q_ref[rows, cols]
            kc = k_ref[rows, cols]
            vc = v_ref[rows, cols]
            ig_c = gcol[:, h:h + 1]
            ig_r = grow[h:h + 1, :]
            lf_c = gcol[:, heads + h:heads + h + 1]
            lf_r = grow[heads + h:heads + h + 1, :]
            bcum_c = jnp.sum(jnp.where(causal, lf_r, 0.0), axis=1, keepdims=True)
            bcum_r = jnp.sum(jnp.where(ti <= si, lf_c, 0.0), axis=0, keepdims=True)
            m_st = m_ref[h:h + 1, 0:1]
            dm = jnp.where(causal, bcum_c - bcum_r + ig_r, -jnp.inf)
            inter = bcum_c + m_st
            m_t = jnp.maximum(inter, jnp.max(dm, axis=1, keepdims=True))
            dexp = jnp.exp(dm - m_t)
            sc = jnp.exp(inter - m_t)
            sco = (_dot_nt(qc, kc) * scale) * dexp
            cst = c_ref[h]
            num = _dot(sco.astype(BF16), vc) + sc * _dot(qc, cst.astype(BF16))
            qn = _dot_nt(qc, nb)[:, h:h + 1]
            den = jnp.sum(sco, axis=1, keepdims=True) + sc * qn
            hc = num * (1.0 / jnp.maximum(jnp.abs(den), jnp.exp(-m_t)))
            m_new = m_t[lc - 1:lc, :]
            b_last = bcum_c[lc - 1:lc, :]
            w_c = jnp.exp(b_last - bcum_c + ig_c - m_new) * scale
            w_r = jnp.exp(b_last - bcum_r + ig_r - m_new) * scale
            decay = jnp.exp(b_last + m_st - m_new)
            kw = kc * w_c.astype(BF16)
            c_ref[h] = decay * cst + _dot_tn(kw, vc)
            wk = _dot(jnp.broadcast_to(w_r, (SUBLANES, lc)).astype(BF16), kc)
            n_ref[h:h + 1, :] = decay * n_ref[h:h + 1, :] + wk[0:1, :]
            m_ref[h:h + 1, :] = jnp.broadcast_to(m_new, (1, LANES))
            hnrm = hc * lax.rsqrt(jnp.mean(hc * hc, axis=-1, keepdims=True) + RMS_EPS) * ng_ref[:, cols]
            zc = z_ref[rows, cols].astype(F32)
            xcc = xc_ref[rows, cols].astype(F32)
            y_ref[rows, cols] = (jax.nn.sigmoid(zc) * (hnrm + skip_ref[:, cols] * xcc)).astype(BF16)
        return carry

    lax.fori_loop(0, tm // lc, chunk_body, 0, unroll=True)
    o_ref[...] = x + _dot(y_ref[...], wout_ref[...])


def _mlstm_layer(x, g, w_in, conv_w, conv_b, w_q, w_k, w_v, w_gate, b_gate, norm_g, skip, w_out, heads,
                 layer):
    s, d = x.shape
    inner = w_out.shape[1]
    tm = min(TOKEN_TILE, s)
    ng = 2 * heads
    wqk = jnp.concatenate([_block_diag_tiles(w_q, MXU_DIM), _block_diag_tiles(w_k, MXU_DIM)],
                          axis=-1).astype(BF16)
    wv = _block_diag_tiles(w_v, MXU_DIM).astype(BF16)
    wgp = jnp.pad(w_gate, ((0, 0), (0, LANES - ng))).astype(BF16)
    wg_parts = [wgp[p * inner:(p + 1) * inner] for p in range(3)]
    bgp = jnp.pad(b_gate, (0, LANES - ng)).reshape(1, LANES)
    nqb = inner // MXU_DIM
    kern = functools.partial(_mlstm_kernel, tm=tm, inner=inner, heads=heads)
    in_specs = [_row_spec(tm, d), _const_spec((1, d)), _layer_spec(w_in, layer),
                _const_spec((CONV_W, inner)), _const_spec((1, inner)),
                _const_spec((nqb, MXU_DIM, 2 * MXU_DIM)), _const_spec((nqb, MXU_DIM, MXU_DIM))]
    in_specs += [_const_spec((inner, LANES))] * 3
    in_specs += [_const_spec((1, LANES)), _const_spec((1, inner)), _const_spec((1, inner)),
                 _layer_spec(w_out, layer)]
    hd = inner // heads
    scratch = [pltpu.VMEM((SUBLANES, inner), F32)]
    scratch += [pltpu.VMEM((tm, inner), BF16)] * 6
    scratch += [pltpu.VMEM((tm, LANES), F32), pltpu.VMEM((tm // ML_CHUNK, SUBLANES, ML_CHUNK), F32),
                pltpu.VMEM((heads, hd, hd), F32), pltpu.VMEM((SUBLANES, hd), F32),
                pltpu.VMEM((SUBLANES, LANES), F32)]
    return pl.pallas_call(
        kern,
        out_shape=jax.ShapeDtypeStruct((s, d), F32),
        grid=(s // tm,),
        in_specs=in_specs,
        out_specs=_row_spec(tm, d),
        scratch_shapes=scratch,
        compiler_params=_cparams(1, 56),
        name="mlstm_layer",
    )(x, g.reshape(1, d), w_in, conv_w, conv_b.reshape(1, inner), wqk, wv, *wg_parts, bgp,
      norm_g.reshape(1, inner), skip.reshape(1, inner), w_out)


def _s5_in_kernel(x_ref, g_ref, o_ref, slab_ref, *, tm):
    hn = _rms(x_ref[...], g_ref[...])
    nslab = hn.shape[1] // LANES
    for k in range(nslab):
        slab_ref[k] = hn[:, k * LANES:(k + 1) * LANES]
    for s in range(S5_CHUNK):
        for k in range(nslab):
            rows = slab_ref[k, pl.ds(s, tm // S5_CHUNK, stride=S5_CHUNK), :]
            o_ref[s, :, k * LANES:(k + 1) * LANES] = rows.astype(BF16)


def _s5_scan_kernel(xs_ref, t_ref, bc_ref, cc_ref, pre_ref, pim_ref, ys_ref,
                    sre_ref, sim_ref, cre_ref, cim_ref, *, cb, half):
    lc = S5_CHUNK

    @pl.when(pl.program_id(1) == 0)
    def _():
        cre_ref[...] = jnp.zeros_like(cre_ref)
        cim_ref[...] = jnp.zeros_like(cim_ref)

    u = jnp.concatenate([xs_ref[s] for s in range(lc)], axis=1)
    npair = (lc * LANES) // MXU_DIM
    ytiles = []
    for b in range(npair):
        acc = _dot(u[:, 0:MXU_DIM], t_ref[b])
        for a in range(1, b + 1):
            acc = acc + _dot(u[:, a * MXU_DIM:(a + 1) * MXU_DIM], t_ref[b - a])
        ytiles.append(acc)
    sinc = _dot(u, bc_ref[...])
    sre_ref[...] = sinc[:, :half]
    sim_ref[...] = sinc[:, half:]
    cin_re, cin_im = cre_ref[...], cim_ref[...]
    pr0, pi0 = pre_ref[0], pim_ref[0]
    sre_ref[0:1, :] = sre_ref[0:1, :] + (pr0 * cin_re - pi0 * cin_im)
    sim_ref[0:1, :] = sim_ref[0:1, :] + (pr0 * cin_im + pi0 * cin_re)
    xre, xim = sre_ref[...], sim_ref[...]
    row = lax.broadcasted_iota(I32, xre.shape, 0)
    d, kstep = 1, 0
    while d < cb:
        keep = row >= d
        re_sh = jnp.where(keep, pltpu.roll(xre, d, 0), 0.0)
        im_sh = jnp.where(keep, pltpu.roll(xim, d, 0), 0.0)
        pr, pi = pre_ref[kstep], pim_ref[kstep]
        xre, xim = xre + (pr * re_sh - pi * im_sh), xim + (pr * im_sh + pi * re_sh)
        d *= 2
        kstep += 1
    cre_ref[...] = xre[cb - 1:cb, :]
    cim_ref[...] = xim[cb - 1:cb, :]
    first = row == 0
    prev_re = jnp.where(first, cin_re, pltpu.roll(xre, 1, 0)).astype(BF16)
    prev_im = jnp.where(first, cin_im, pltpu.roll(xim, 1, 0)).astype(BF16)
    prev = jnp.concatenate([prev_re, prev_im], axis=1)
    for b in range(npair):
        yb = ytiles[b] + _dot(prev, cc_ref[:, b * MXU_DIM:(b + 1) * MXU_DIM])
        ys_ref[2 * b] = yb[:, :LANES]
        ys_ref[2 * b + 1] = yb[:, LANES:]


def _cmul(ar, ai, br, bi):
    return ar * br - ai * bi, ar * bi + ai * br


def _s5_operators(a_re, a_im, log_step, b_re, b_im, c_re, c_im, cb):
    lc = S5_CHUNK
    ng, np_, ni = b_re.shape
    gpv = LANES // ni
    nv = ng // gpv
    npair = lc // 2
    step = jnp.exp(log_step)[:, None]
    taus = jnp.arange(lc + 1, dtype=F32)[None, :, None]
    mag = jnp.exp((a_re * step)[:, None, :] * taus)
    ang = (a_im * step)[:, None, :] * taus
    pr, pi = mag * jnp.cos(ang), mag * jnp.sin(ang)
    lr, li = pr[:, 1], pi[:, 1]
    den = a_re * a_re + a_im * a_im
    fr, fi = _cmul(lr - 1.0, li, a_re / den, -a_im / den)
    bbr, bbi = _cmul(fr[..., None], fi[..., None], b_re, b_im)
    bbrt, bbit = bbr.transpose(0, 2, 1), bbi.transpose(0, 2, 1)
    mr, mi = _cmul(c_re[:, None], c_im[:, None], pr[:, :, None, :], pi[:, :, None, :])
    kk = jnp.sum(mr[:, :lc, None, :, :] * bbrt[:, None, :, None, :]
                 - mi[:, :lc, None, :, :] * bbit[:, None, :, None, :], axis=-1)
    grp16 = lambda r: (r // ni) % gpv
    lag_of = np.zeros((npair, 2, 2), np.int64)
    for dd in range(npair):
        for s2 in range(2):
            for t2 in range(2):
                lag_of[dd, s2, t2] = 2 * dd + t2 - s2
    sel = (np.arange(lc)[:, None, None, None] == lag_of[None]).astype(np.float32)
    sp_t = np.einsum('ldst,jk->ljdstk', sel, np.eye(ni, dtype=np.float32))
    sp_t = np.broadcast_to(sp_t[:, :, :, :, :, None, :], (lc, ni, npair, 2, 2, gpv, ni))
    sp_t = jnp.asarray(sp_t.reshape(lc * ni, npair * 2 * 2 * gpv * ni))
    kk2 = kk.transpose(0, 2, 1, 3).reshape(ng * ni, lc * ni)
    toep = _spread_masked(kk2, sp_t, grp16, grp16)
    toep = toep.reshape(nv, gpv * ni, npair * 2, 2 * gpv * ni).transpose(0, 2, 1, 3)
    toep = toep.reshape(nv, npair, MXU_DIM, MXU_DIM)
    rtaus = (lc - 1) - taus[:, :lc]
    rmag = jnp.exp((a_re * step)[:, None, :] * rtaus)
    rang = (a_im * step)[:, None, :] * rtaus
    rev_r, rev_i = rmag * jnp.cos(rang), rmag * jnp.sin(rang)
    bcr, bci = _cmul(rev_r[:, :, None, :], rev_i[:, :, None, :], bbrt[:, None], bbit[:, None])
    bcx = jnp.concatenate([bcr, bci], axis=-1).reshape(nv, gpv, lc, ni, 2 * np_).transpose(0, 2, 1, 3, 4)
    sp_b = jnp.tile(jnp.eye(2 * np_, dtype=F32).reshape(2 * np_, 2, 1, np_), (1, 1, gpv, 1)).reshape(2 * np_, -1)
    bc = _spread_masked(bcx.reshape(-1, 2 * np_), sp_b, grp16, lambda c: (c // np_) % gpv)
    bc = bc.reshape(nv, lc * gpv * ni, 2 * gpv * np_)
    ccx = jnp.stack([mr[:, 1:lc + 1], -mi[:, 1:lc + 1]], axis=1)
    ccx = ccx.reshape(nv, gpv, 2, lc, ni, np_).transpose(0, 2, 1, 5, 3, 4)
    sp_c = jnp.tile(jnp.eye(lc * ni, dtype=F32).reshape(lc * ni, lc, 1, ni), (1, 1, gpv, 1)).reshape(lc * ni, -1)
    cc = _spread_masked(ccx.reshape(-1, lc * ni), sp_c, lambda r: (r // np_) % gpv, grp16)
    cc = cc.reshape(nv, 2 * gpv * np_, lc * gpv * ni)
    nsteps = max(1, int(math.ceil(math.log2(max(cb, 2)))))
    sq = jnp.stack([pr[:, lc].reshape(1, -1), pi[:, lc].reshape(1, -1)])
    scan = []
    for _ in range(nsteps):
        scan.append(sq)
        sq = jnp.stack(_cmul(sq[0], sq[1], sq[0], sq[1]))
    scan = jnp.stack(scan, axis=1)
    return toep, bc, cc, scan[0], scan[1]


def _s5_out_kernel(x_ref, ys_ref, g_ref, d_ref, wglu_ref, bglu_ref, o_ref, slab_ref, *, tm):
    x = x_ref[...]
    nslab = x.shape[1] // LANES
    for s in range(S5_CHUNK):
        for k in range(nslab):
            slab_ref[k, pl.ds(s, tm // S5_CHUNK, stride=S5_CHUNK), :] = ys_ref[s, :, k * LANES:(k + 1) * LANES]
    yt = jnp.concatenate([slab_ref[k] for k in range(nslab)], axis=1)
    hn = _rms(x, g_ref[...])
    y = _gelu_tanh(yt + d_ref[...] * hn)
    o_ref[...] = x + y * jax.nn.sigmoid(_dot(y.astype(BF16), wglu_ref[...]) + bglu_ref[...])


def _s5_layer(x, g, a_re, a_im, log_step, b_re, b_im, c_re, c_im, d_skip, w_glu, b_glu, layer):
    s, d = x.shape
    ngroups, pstate, gch = b_re.shape
    lc = S5_CHUNK
    nchunk = s // lc
    cb = min(S5_CHUNK_BLOCK, nchunk)
    gpv = LANES // gch
    nv = ngroups // gpv
    half = gpv * pstate
    toep, bc, cc, scan_r, scan_i = _s5_operators(a_re, a_im, log_step, b_re, b_im, c_re, c_im, cb)
    nsteps = scan_r.shape[0]
    tm = min(TOKEN_TILE, s)
    nslab = d // LANES
    step_major = pl.BlockSpec((lc, tm // lc, d), lambda i: (0, i, 0))
    xs = pl.pallas_call(
        functools.partial(_s5_in_kernel, tm=tm),
        out_shape=jax.ShapeDtypeStruct((lc, nchunk, d), BF16),
        grid=(s // tm,),
        in_specs=[_row_spec(tm, d), _const_spec((1, d))],
        out_specs=step_major,
        scratch_shapes=[pltpu.VMEM((nslab, tm, LANES), F32)],
        compiler_params=_cparams(1, 24),
        name="s5_in",
    )(x, g.reshape(1, d))
    wl = lc * LANES
    npair = lc // 2
    ys = pl.pallas_call(
        functools.partial(_s5_scan_kernel, cb=cb, half=half),
        out_shape=jax.ShapeDtypeStruct((lc, nchunk, d), F32),
        grid=(nv, nchunk // cb),
        in_specs=[pl.BlockSpec((lc, cb, LANES), lambda v, j: (0, j, v)),
                  pl.BlockSpec((None, npair, MXU_DIM, MXU_DIM), lambda v, j: (v, 0, 0, 0)),
                  pl.BlockSpec((None, wl, 2 * half), lambda v, j: (v, 0, 0)),
                  pl.BlockSpec((None, 2 * half, wl), lambda v, j: (v, 0, 0)),
                  pl.BlockSpec((nsteps, 1, half), lambda v, j: (0, 0, v)),
                  pl.BlockSpec((nsteps, 1, half), lambda v, j: (0, 0, v))],
        out_specs=pl.BlockSpec((lc, cb, LANES), lambda v, j: (0, j, v)),
        scratch_shapes=[pltpu.VMEM((cb, half), F32), pltpu.VMEM((cb, half), F32),
                        pltpu.VMEM((1, half), F32), pltpu.VMEM((1, half), F32)],
        compiler_params=_cparams(2, 52),
        name="s5_scan",
    )(xs, toep, bc, cc, scan_r, scan_i)
    return pl.pallas_call(
        functools.partial(_s5_out_kernel, tm=tm),
        out_shape=jax.ShapeDtypeStruct((s, d), F32),
        grid=(s // tm,),
        in_specs=[_row_spec(tm, d), step_major, _const_spec((1, d)), _const_spec((1, d)),
                  _layer_spec(w_glu, layer), _const_spec((1, d))],
        out_specs=_row_spec(tm, d),
        scratch_shapes=[pltpu.VMEM((nslab, tm, LANES), F32)],
        compiler_params=_cparams(1, 32),
        name="s5_out",
    )(x, ys, g.reshape(1, d), d_skip.reshape(1, d), w_glu, b_glu.reshape(1, d))


def _router_kernel(x_ref, g_ref, wr_ref, br_ref, hn_ref, wts_ref, idx_ref, cnt_ref, run_ref, *, tm, ne):
    @pl.when(pl.program_id(0) == 0)
    def _():
        run_ref[...] = jnp.zeros_like(run_ref)

    hn = _rms(x_ref[...], g_ref[...])
    for k in range(hn.shape[1] // LANES):
        hn_ref[pl.ds(k, tm, stride=SUBLANES), :] = hn[:, k * LANES:(k + 1) * LANES]
    wr = wr_ref[...]
    hn_hi, wr_hi = hn.astype(BF16), wr.astype(BF16)
    hn_lo, wr_lo = (hn - hn_hi.astype(F32)).astype(BF16), (wr - wr_hi.astype(F32)).astype(BF16)
    logits = (_dot(hn_hi, wr_hi) + _dot(hn_hi, wr_lo)) + (_dot(hn_lo, wr_hi) + _dot(hn_lo, wr_lo))
    logits = logits + br_ref[...]
    lane = lax.broadcasted_iota(I32, (tm, ne), 1).astype(F32)
    m1 = jnp.max(logits, axis=1, keepdims=True)
    i1 = jnp.min(jnp.where(logits == m1, lane, float(ne)), axis=1, keepdims=True)
    sel1 = lane == i1
    rest = jnp.where(sel1, -jnp.inf, logits)
    m2 = jnp.max(rest, axis=1, keepdims=True)
    i2 = jnp.min(jnp.where(rest == m2, lane, float(ne)), axis=1, keepdims=True)
    sel2 = lane == i2
    e = jnp.exp(m2 - m1)
    w1 = 1.0 / (1.0 + e)
    w2 = e / (1.0 + e)
    sel = jnp.logical_or(sel1, sel2)
    ti = lax.broadcasted_iota(I32, (tm, tm), 0)
    si = lax.broadcasted_iota(I32, (tm, tm), 1)
    earlier = jnp.where(si < ti, 1.0, 0.0).astype(BF16)
    excl = _dot(earlier, jnp.where(sel, 1.0, 0.0).astype(BF16)) + run_ref[...]
    r1 = jnp.sum(jnp.where(sel1, excl, 0.0), axis=1, keepdims=True)
    r2 = jnp.sum(jnp.where(sel2, excl, 0.0), axis=1, keepdims=True)
    total = run_ref[...] + jnp.sum(jnp.where(sel, 1.0, 0.0), axis=0, keepdims=True)
    run_ref[...] = total
    cnt_ref[...] = total
    wts_ref[...] = jnp.where(lane == 0.0, w1, jnp.where(lane == 1.0, w2, 0.0))
    lane_w = lax.broadcasted_iota(I32, (tm, LANES), 1)
    meta = jnp.zeros((tm, LANES), F32)
    for c, val in enumerate([i1, i2, r1, r2]):
        meta = jnp.where(lane_w == c, val, meta)
    idx_ref[...] = meta.T[0:SUBLANES, :].astype(I32)


def _token_rows(t):
    return pl.ds(pl.multiple_of(t * SUBLANES, SUBLANES), SUBLANES)


def _untile_tokens(ref, n, nslab, lead=()):
    return jnp.concatenate([ref[lead + (pl.ds(k, n, stride=SUBLANES), slice(None))] for k in range(nslab)], axis=1)


def _dispatch_kernel(ends_ref, pos_ref, hn_ref, xs_ref, ring_ref, zero_ref, in_sem, out_sem, zsem, *, tg, tr, ne):
    i = pl.program_id(0)
    n = pl.num_programs(0)
    nrows = tg * SUBLANES

    def fetch(step, sl):
        src = hn_ref.at[pl.ds(pl.multiple_of(step * nrows, nrows), nrows), :]
        return pltpu.make_async_copy(src, ring_ref.at[sl], in_sem.at[sl])

    def row_copy(t, dst_row, sl):
        return pltpu.make_async_copy(ring_ref.at[sl, _token_rows(t), :], xs_ref.at[_token_rows(dst_row), :],
                                     out_sem.at[sl])

    def drain(sl):
        def body(t, carry):
            for k in range(TOP_K):
                row_copy(t, 0, sl).wait()
            return carry
        lax.fori_loop(0, tg, body, 0, unroll=8)

    @pl.when(i == 0)
    def _():
        fetch(0, 0).start()
        zero_ref[...] = jnp.zeros_like(zero_ref)
        last_tile = xs_ref.shape[0] // SUBLANES - tr
        for e in range(ne):
            for r0 in (jnp.maximum(ends_ref[e] - tr, 0), jnp.minimum(ends_ref[ne - 1] + e * tr, last_tile)):
                dst = xs_ref.at[pl.ds(pl.multiple_of(r0 * SUBLANES, tr * SUBLANES), tr * SUBLANES), :]
                cp = pltpu.make_async_copy(zero_ref, dst, zsem)
                cp.start()
                cp.wait()

    slot = i % 3
    fetch(i, slot).wait()

    @pl.when(i >= 2)
    def _():
        drain((i + 1) % 3)

    @pl.when(i + 1 < n)
    def _():
        fetch(i + 1, (i + 1) % 3).start()

    def issue(t, carry):
        for k in range(TOP_K):
            row_copy(t, pos_ref[k, t], slot).start(priority=k)
        return carry

    lax.fori_loop(0, tg, issue, 0, unroll=8)

    @pl.when(i == n - 1)
    def _():
        @pl.when(i >= 1)
        def _():
            drain((i + 2) % 3)
        drain(slot)


def _expert_kernel(te_ref, nv_ref, xs_ref, w1_ref, w3_ref, w2_ref, ys_ref, *, tr, ff_chunk):
    i = pl.program_id(0)
    nslab = w2_ref.shape[1] // LANES

    @pl.when(i < nv_ref[0])
    def _():
        xb = _untile_tokens(xs_ref, tr, nslab).astype(BF16)
        y = None
        for c in range(w1_ref.shape[1] // ff_chunk):
            sl = slice(c * ff_chunk, (c + 1) * ff_chunk)
            h = (_silu(_dot(xb, w1_ref[:, sl])) * _dot(xb, w3_ref[:, sl])).astype(BF16)
            part = _dot(h, w2_ref[sl, :])
            y = part if y is None else y + part
        for k in range(nslab):
            ys_ref[pl.ds(k, tr, stride=SUBLANES), :] = y[:, k * LANES:(k + 1) * LANES]

    @pl.when(i >= nv_ref[0])
    def _():
        ys_ref[...] = jnp.zeros_like(ys_ref)


def _combine_kernel(pos_ref, nxt_ref, x_ref, wts_ref, gf_ref, ys_ref, o_ref, buf_ref, sem, *, tg, final_norm):
    i = pl.program_id(0)
    slot = i % 2

    def row_copy(t, k, src_row, sl):
        return pltpu.make_async_copy(ys_ref.at[_token_rows(src_row), :], buf_ref.at[sl, k, _token_rows(t), :],
                                     sem.at[sl])

    def request(rows_ref, sl):
        def issue(t, carry):
            for k in range(TOP_K):
                row_copy(t, k, rows_ref[k, t], sl).start(priority=k)
            return carry
        lax.fori_loop(0, tg, issue, 0, unroll=8)

    @pl.when(i == 0)
    def _():
        request(pos_ref, 0)

    @pl.when(i + 1 < pl.num_programs(0))
    def _():
        request(nxt_ref, 1 - slot)

    def drain(t, carry):
        for k in range(TOP_K):
            row_copy(t, k, 0, slot).wait()
        return carry

    lax.fori_loop(0, tg, drain, 0, unroll=8)

    x = x_ref[...]
    nslab = x.shape[1] // LANES
    wts = wts_ref[...]
    out = x + (wts[:, 0:1] * _untile_tokens(buf_ref, tg, nslab, (slot, 0))
               + wts[:, 1:2] * _untile_tokens(buf_ref, tg, nslab, (slot, 1)))
    if final_norm:
        out = _rms(out, gf_ref[...])
    o_ref[...] = out


def _moe_layer(x, g, router, router_b, w1, w3, w2, layer, g_final, final_norm):
    s, d = x.shape
    ne = router.shape[1]
    f = w1.shape[3]
    tm = min(TOKEN_TILE, s)
    tg = min(GATHER_TILE, s)
    tr = GROUP_TILE
    sub = SUBLANES
    hn, wts, idx, cnt = pl.pallas_call(
        functools.partial(_router_kernel, tm=tm, ne=ne),
        out_shape=(jax.ShapeDtypeStruct((s * sub, LANES), F32), jax.ShapeDtypeStruct((s, ne), F32),
                   jax.ShapeDtypeStruct((sub, s), I32), jax.ShapeDtypeStruct((1, ne), F32)),
        grid=(s // tm,),
        in_specs=[_row_spec(tm, d), _const_spec((1, d)), _const_spec((d, ne)), _const_spec((1, ne))],
        out_specs=(_row_spec(tm * sub, LANES), _row_spec(tm, ne), pl.BlockSpec((sub, tm), lambda i: (0, i)),
                   pl.BlockSpec((1, ne), lambda i: (0, 0))),
        scratch_shapes=[pltpu.VMEM((1, ne), F32)],
        compiler_params=_cparams(1, 24),
        name="moe_router",
    )(x, g.reshape(1, d), router, router_b.reshape(1, ne))

    counts = cnt[0].astype(I32)
    padded = ((counts + tr - 1) // tr) * tr
    ends = jnp.cumsum(padded)
    starts = ends - padded
    n_tiles = (TOP_K * s) // tr + ne
    rows = n_tiles * tr
    n_valid = ends[-1] // tr
    tile_row = jnp.minimum(jnp.arange(n_tiles, dtype=I32), n_valid - 1) * tr
    tile_expert = jnp.minimum(jnp.sum((ends[None, :] <= tile_row[:, None]).astype(I32), axis=1), ne - 1)

    expert_start = jnp.sum(jnp.where(idx[0:TOP_K, :, None] == jnp.arange(ne, dtype=I32), starts, 0), axis=-1)
    pos = expert_start + idx[TOP_K:2 * TOP_K]

    pos_spec = pl.BlockSpec((TOP_K, tg), lambda i, *_: (0, i), memory_space=pltpu.SMEM)
    any_spec = pl.BlockSpec(memory_space=pl.ANY)
    xs = pl.pallas_call(
        functools.partial(_dispatch_kernel, tg=tg, tr=tr, ne=ne),
        out_shape=jax.ShapeDtypeStruct((rows * sub, LANES), F32),
        grid_spec=pltpu.PrefetchScalarGridSpec(
            num_scalar_prefetch=1,
            grid=(s // tg,),
            in_specs=[pos_spec, any_spec],
            out_specs=any_spec,
            scratch_shapes=[pltpu.VMEM((3, tg * sub, LANES), F32), pltpu.VMEM((tr * sub, LANES), F32),
                            pltpu.SemaphoreType.DMA((3,)), pltpu.SemaphoreType.DMA((3,)),
                            pltpu.SemaphoreType.DMA],
        ),
        compiler_params=_cparams(1, 16),
        name="moe_dispatch",
    )(ends, pos, hn)

    row_tile = lambda i, te, nv: (jnp.maximum(jnp.minimum(i, nv[0] - 1), 0), 0)
    expert_w = lambda i, te, nv: (layer, te[i], 0, 0)
    ys = pl.pallas_call(
        functools.partial(_expert_kernel, tr=tr, ff_chunk=MXU_DIM),
        out_shape=jax.ShapeDtypeStruct((rows * sub, LANES), F32),
        grid_spec=pltpu.PrefetchScalarGridSpec(
            num_scalar_prefetch=2,
            grid=(n_tiles,),
            in_specs=[pl.BlockSpec((tr * sub, LANES), row_tile),
                      pl.BlockSpec((None, None, d, f), expert_w),
                      pl.BlockSpec((None, None, d, f), expert_w),
                      pl.BlockSpec((None, None, f, d), expert_w)],
            out_specs=pl.BlockSpec((tr * sub, LANES), lambda i, te, nv: (i, 0)),
        ),
        compiler_params=_cparams(1, 54),
        name="moe_experts",
    )(tile_expert, n_valid.reshape(1), xs, w1, w3, w2)

    return pl.pallas_call(
        functools.partial(_combine_kernel, tg=tg, final_norm=final_norm),
        out_shape=jax.ShapeDtypeStruct((s, d), F32),
        grid=(s // tg,),
        in_specs=[pl.BlockSpec((TOP_K, tg), lambda i: (0, i), memory_space=pltpu.SMEM),
                  pl.BlockSpec((TOP_K, tg), lambda i: (0, jnp.minimum(i + 1, s // tg - 1)),
                               memory_space=pltpu.SMEM),
                  _row_spec(tg, d), _row_spec(tg, ne), _const_spec((1, d)), any_spec],
        out_specs=_row_spec(tg, d),
        scratch_shapes=[pltpu.VMEM((2, TOP_K, tg * sub, LANES), F32), pltpu.SemaphoreType.DMA((2,))],
        compiler_params=_cparams(1, 24),
        name="moe_combine",
    )(pos, pos, x, wts, g_final.reshape(1, d), ys)


def _norm_kernel(x_ref, g_ref, o_ref):
    o_ref[...] = _rms(x_ref[...], g_ref[...])


def _final_norm(x, g):
    s, d = x.shape
    tm = min(TOKEN_TILE, s)
    return pl.pallas_call(
        _norm_kernel,
        out_shape=jax.ShapeDtypeStruct((s, d), F32),
        grid=(s // tm,),
        in_specs=[_row_spec(tm, d), _const_spec((1, d))],
        out_specs=_row_spec(tm, d),
        compiler_params=_cparams(1, 16),
        name="final_norm",
    )(x, g.reshape(1, d))


def kernel(x, norm_mix, norm_ffn, norm_final, a_w_in, a_conv_w, a_conv_b, a_w_r, a_b_r, a_w_i, a_b_i, a_lam, a_w_out, b_w_in, b_conv_w, b_conv_b, b_w_q, b_w_k, b_w_v, b_w_gate, b_b_gate, b_norm, b_skip, b_w_out, c_a_re, c_a_im, c_log_step, c_b_re, c_b_im, c_c_re, c_c_im, c_d, c_w_glu, c_b_glu, f_w1, f_w3, f_w2, e_router, e_router_b, e_w1, e_w3, e_w2):
    bsz, seq, d = x.shape
    depth = norm_mix.shape[0]
    heads = b_b_gate.shape[1] // 2
    a_w_in, a_w_out, b_w_in, b_w_out, c_w_glu = (w.astype(BF16) for w in (a_w_in, a_w_out, b_w_in, b_w_out,
                                                                            c_w_glu))
    f_w1, f_w3, f_w2, e_w1, e_w3, e_w2 = (w.astype(BF16) for w in (f_w1, f_w3, f_w2, e_w1, e_w3, e_w2))
    outs = []
    for bi in range(bsz):
        h = x[bi]
        for i in range(depth):
            kind, j = i % 3, i // 3
            if kind == 0:
                h = _rglru_layer(h, norm_mix[i], a_w_in, a_conv_w[j], a_conv_b[j], a_w_r[j], a_b_r[j],
                                 a_w_i[j], a_b_i[j], a_lam[j], a_w_out, j)
            elif kind == 1:
                h = _mlstm_layer(h, norm_mix[i], b_w_in, b_conv_w[j], b_conv_b[j], b_w_q[j], b_w_k[j],
                                 b_w_v[j], b_w_gate[j], b_b_gate[j], b_norm[j], b_skip[j], b_w_out, heads, j)
            else:
                h = _s5_layer(h, norm_mix[i], c_a_re[j], c_a_im[j], c_log_step[j], c_b_re[j], c_b_im[j],
                              c_c_re[j], c_c_im[j], c_d[j], c_w_glu, c_b_glu[j], j)
            fidx = i // 2
            last = i == depth - 1
            if i % 2 == 0:
                h = _ffn_dense(h, norm_ffn[i], f_w1, f_w3, f_w2, fidx)
                if last:
                    h = _final_norm(h, norm_final)
            else:
                h = _moe_layer(h, norm_ffn[i], e_router[fidx], e_router_b[fidx], e_w1, e_w3, e_w2, fidx,
                               norm_final, last)
        outs.append(h)
    return jnp.stack(outs)
```

```python
import functools
import math

import jax
import jax.numpy as jnp
import numpy as np
from jax import lax
from jax.experimental import pallas as pl
from jax.experimental.pallas import tpu as pltpu

F32 = jnp.float32
BF16 = jnp.bfloat16
I32 = jnp.int32

RMS_EPS = 1e-6
CONV_W = 4
RG_C = 8.0
ML_CHUNK = 128
TOP_K = 2

V7X_VMEM_BYTES = 64 * 1024 * 1024
SUBLANES = 8
LANES = 128
MXU_DIM = 256

TOKEN_TILE = 512
GATHER_TILE = 256
GROUP_TILE = 512
S5_CHUNK = 16
S5_CHUNK_BLOCK = 512


def _cparams(n_axes, vmem_mib):
    return pltpu.CompilerParams(
        dimension_semantics=("arbitrary",) * n_axes,
        vmem_limit_bytes=min(vmem_mib * 1024 * 1024, V7X_VMEM_BYTES - 6 * 1024 * 1024),
    )


def _const_spec(shape):
    nd = len(shape)
    return pl.BlockSpec(shape, lambda *_: (0,) * nd, pipeline_mode=pl.Buffered(1))


def _layer_spec(stacked, layer):
    nd = stacked.ndim
    return pl.BlockSpec((None,) + tuple(stacked.shape[1:]), lambda *_: (layer,) + (0,) * (nd - 1),
                        pipeline_mode=pl.Buffered(1))


def _row_spec(tm, d):
    return pl.BlockSpec((tm, d), lambda i: (i, 0))


def _rms(x, g):
    return x * lax.rsqrt(jnp.mean(x * x, axis=-1, keepdims=True) + RMS_EPS) * g


def _dot(a, b):
    return jnp.dot(a, b, preferred_element_type=F32)


def _dot_nt(a, b):
    return lax.dot_general(a, b, (((1,), (1,)), ((), ())), preferred_element_type=F32)


def _dot_tn(a, b):
    return lax.dot_general(a, b, (((0,), (0,)), ((), ())), preferred_element_type=F32)


def _silu(x):
    return x * jax.nn.sigmoid(x)


def _gelu_tanh(x):
    return 0.5 * x * (1.0 + jnp.tanh(math.sqrt(2.0 / math.pi) * (x + 0.044715 * (x * x * x))))


def _log_sigmoid(x):
    return jnp.minimum(x, 0.0) - jnp.log1p(jnp.exp(-jnp.abs(x)))


def _causal_conv(tail_ref, xin, cw, cb, cols, tm):
    prev = tail_ref[:, cols]
    sub = lax.broadcasted_iota(I32, prev.shape, 0)
    out = cb
    for d in (3, 2, 1):
        sh = pltpu.roll(xin, d, 0)
        head = jnp.where(sub < d, pltpu.roll(prev, d, 0), sh[0:SUBLANES])
        out = out + cw[3 - d:4 - d] * jnp.concatenate([head, sh[SUBLANES:]], axis=0)
    out = out + cw[3:4] * xin
    tail_ref[:, cols] = xin[tm - SUBLANES:tm, :]
    return out


def _spread_masked(x, spread, row_group, col_group):
    y = jnp.dot(x.astype(BF16), spread.astype(BF16), preferred_element_type=F32)
    keep = row_group(jnp.arange(x.shape[0]))[:, None] == col_group(jnp.arange(spread.shape[1]))[None, :]
    return jnp.where(keep, y, 0.0).astype(BF16)


def _block_diag_tiles(w, tile):
    nb, k, _ = w.shape
    per = tile // k
    spread = jnp.tile(jnp.eye(k, dtype=F32), (1, per))
    tiles = _spread_masked(w.reshape(nb * k, k), spread, lambda r: (r // k) % per, lambda c: c // k)
    return tiles.reshape(nb // per, tile, tile)


def _cast_plan(jobs, steps):
    in_specs, out_specs, out_shapes, args = [], [], [], []
    for arr, layer_rows, layer in jobs:
        blk = layer_rows // steps
        cols = arr.shape[1]
        base = layer * steps
        in_specs.append(pl.BlockSpec((blk, cols), lambda i, base=base: (base + i, 0)))
        out_specs.append(pl.BlockSpec((blk, cols), lambda i: (i, 0)))
        out_shapes.append(jax.ShapeDtypeStruct((layer_rows, cols), BF16))
        args.append(arr)
    return in_specs, out_specs, out_shapes, args


def _run_casts(cast_refs):
    n = len(cast_refs) // 2
    for src, dst in zip(cast_refs[:n], cast_refs[n:]):
        dst[...] = src[...].astype(BF16)


def _ffn_dense_kernel(x_ref, g_ref, w1_ref, w3_ref, w2_ref, *rest, ff_chunk, ncast):
    o_ref = rest[ncast]
    x = x_ref[...]
    hn = _rms(x, g_ref[...]).astype(BF16)
    acc = x
    for c in range(w1_ref.shape[1] // ff_chunk):
        sl = slice(c * ff_chunk, (c + 1) * ff_chunk)
        h = (_silu(_dot(hn, w1_ref[:, sl])) * _dot(hn, w3_ref[:, sl])).astype(BF16)
        acc = acc + _dot(h, w2_ref[sl, :])
    o_ref[...] = acc
    _run_casts(rest[:ncast] + rest[ncast + 1:])


def _ffn_dense(x, g, w1, w3, w2, layer, cast_jobs=()):
    s, d = x.shape
    tm = min(TOKEN_TILE, s)
    steps = s // tm
    c_in, c_out, c_shapes, c_args = _cast_plan(cast_jobs, steps)
    outs = pl.pallas_call(
        functools.partial(_ffn_dense_kernel, ff_chunk=MXU_DIM, ncast=len(cast_jobs)),
        out_shape=[jax.ShapeDtypeStruct((s, d), F32)] + c_shapes,
        grid=(steps,),
        in_specs=[_row_spec(tm, d), _const_spec((1, d)), _layer_spec(w1, layer), _layer_spec(w3, layer),
                  _layer_spec(w2, layer)] + c_in,
        out_specs=[_row_spec(tm, d)] + c_out,
        compiler_params=_cparams(1, 56),
        name="ffn_dense",
    )(x, g.reshape(1, d), w1, w3, w2, *c_args)
    return outs[0], list(outs[1:])


def _linear_scan_rows(a, b, h0):
    tm, c = a.shape
    ngrp = tm // SUBLANES
    a3 = a.reshape(ngrp, SUBLANES, c)
    b3 = b.reshape(ngrp, SUBLANES, c)
    sub = lax.broadcasted_iota(I32, (1, SUBLANES, c), 1)
    for d in (1, 2, 4):
        keep = sub >= d
        a_sh = jnp.where(keep, pltpu.roll(a3, d, 1), 1.0)
        b_sh = jnp.where(keep, pltpu.roll(b3, d, 1), 0.0)
        b3 = a3 * b_sh + b3
        a3 = a3 * a_sh
    carry = h0
    rows = []
    for j in range(ngrp):
        hj = b3[j] + a3[j] * carry
        rows.append(hj)
        carry = hj[SUBLANES - 1:SUBLANES, :]
    return jnp.concatenate(rows, axis=0), carry


def _rglru_kernel(x_ref, g_ref, win_ref, cw_ref, cb_ref, wg_ref, bg_ref, nsp_ref, wout_ref, *rest,
                  tm, width, ncast):
    o_ref = rest[ncast]
    tail_ref, h_ref = rest[2 * ncast + 1:]
    bd = MXU_DIM

    @pl.when(pl.program_id(0) == 0)
    def _():
        tail_ref[...] = jnp.zeros_like(tail_ref)
        h_ref[...] = jnp.zeros_like(h_ref)

    x = x_ref[...]
    hn = _rms(x, g_ref[...]).astype(BF16)
    ys = []
    for n in range(width // bd):
        cols = slice(n * bd, (n + 1) * bd)
        gx = _dot(hn, win_ref[:, cols])
        xr = _dot(hn, win_ref[:, width + n * bd:width + (n + 1) * bd])
        xr = _causal_conv(tail_ref, xr, cw_ref[:, cols], cb_ref[:, cols], cols, tm)
        gates = _dot(xr.astype(BF16), wg_ref[n]) + bg_ref[n]
        r = jax.nn.sigmoid(gates[:, :bd])
        ig = jax.nn.sigmoid(gates[:, bd:])
        a = jnp.exp(r * nsp_ref[:, cols])
        v = 1.0 - a * a
        b = jnp.where(v > 0.0, v * lax.rsqrt(v), 0.0) * (ig * xr)
        h, last = _linear_scan_rows(a, b, h_ref[:, cols])
        h_ref[:, cols] = last
        ys.append((_gelu_tanh(gx) * h).astype(BF16))
    y = jnp.concatenate(ys, axis=1)
    o_ref[...] = x + _dot(y, wout_ref[...])
    _run_casts(rest[:ncast] + rest[ncast + 1:2 * ncast + 1])


def _rglru_layer(x, g, w_in, conv_w, conv_b, w_r, b_r, w_i, b_i, lam, w_out, layer, cast_jobs=()):
    s, d = x.shape
    width = w_out.shape[1]
    bd = MXU_DIM
    nblk = width // bd
    tm = min(TOKEN_TILE, s)
    wg = jnp.concatenate([_block_diag_tiles(w_r, bd), _block_diag_tiles(w_i, bd)], axis=-1).astype(BF16)
    bg = jnp.concatenate([b_r.reshape(nblk, 1, bd), b_i.reshape(nblk, 1, bd)], axis=-1)
    nsp = (-RG_C * jax.nn.softplus(-lam)).reshape(1, width)
    steps = s // tm
    c_in, c_out, c_shapes, c_args = _cast_plan(cast_jobs, steps)
    kern = functools.partial(_rglru_kernel, tm=tm, width=width, ncast=len(cast_jobs))
    outs = pl.pallas_call(
        kern,
        out_shape=[jax.ShapeDtypeStruct((s, d), F32)] + c_shapes,
        grid=(steps,),
        in_specs=[_row_spec(tm, d), _const_spec((1, d)), _layer_spec(w_in, layer),
                  _const_spec((CONV_W, width)), _const_spec((1, width)),
                  _const_spec((nblk, bd, 2 * bd)), _const_spec((nblk, 1, 2 * bd)),
                  _const_spec((1, width)), _layer_spec(w_out, layer)] + c_in,
        out_specs=[_row_spec(tm, d)] + c_out,
        scratch_shapes=[pltpu.VMEM((SUBLANES, width), F32), pltpu.VMEM((1, width), F32)],
        compiler_params=_cparams(1, 44),
        name="rglru_layer",
    )(x, g.reshape(1, d), w_in, conv_w, conv_b.reshape(1, width), wg, bg, nsp, w_out, *c_args)
    return outs[0], list(outs[1:])


def _mlstm_kernel(x_ref, g_ref, win_ref, cw_ref, cb_ref, wqk_ref, wv_ref, wgq_ref, wgk_ref, wgv_ref,
                  bg_ref, ng_ref, skip_ref, wout_ref, o_ref,
                  tail_ref, q_ref, k_ref, v_ref, xc_ref, z_ref, y_ref, gcol_ref, grow_ref,
                  c_ref, n_ref, m_ref, *, tm, inner, heads):
    hd = inner // heads
    lc = ML_CHUNK
    nqb = inner // MXU_DIM
    scale = hd ** -0.5

    @pl.when(pl.program_id(0) == 0)
    def _():
        tail_ref[...] = jnp.zeros_like(tail_ref)
        c_ref[...] = jnp.zeros_like(c_ref)
        n_ref[...] = jnp.zeros_like(n_ref)
        m_ref[...] = jnp.zeros_like(m_ref)

    x = x_ref[...]
    hn = _rms(x, g_ref[...]).astype(BF16)

    for b in range(nqb):
        cols = slice(b * MXU_DIM, (b + 1) * MXU_DIM)
        xm = _dot(hn, win_ref[:, cols])
        xc = _silu(_causal_conv(tail_ref, xm, cw_ref[:, cols], cb_ref[:, cols], cols, tm))
        xcb = xc.astype(BF16)
        qk = _dot(xcb, wqk_ref[b])
        q_ref[:, cols] = qk[:, :MXU_DIM].astype(BF16)
        k_ref[:, cols] = qk[:, MXU_DIM:].astype(BF16)
        v_ref[:, cols] = _dot(xm.astype(BF16), wv_ref[b]).astype(BF16)
        xc_ref[:, cols] = xcb
        z_ref[:, cols] = _dot(hn, win_ref[:, inner + b * MXU_DIM:inner + (b + 1) * MXU_DIM]).astype(BF16)

    gc = (_dot(q_ref[...], wgq_ref[...]) + _dot(k_ref[...], wgk_ref[...]) + _dot(v_ref[...], wgv_ref[...])
          + bg_ref[...])
    lane = lax.broadcasted_iota(I32, gc.shape, 1)
    gc = jnp.where(lane < heads, gc, _log_sigmoid(gc))
    gcol_ref[...] = gc
    gr = gc.T
    for c in range(tm // lc):
        grow_ref[c] = gr[0:SUBLANES, c * lc:(c + 1) * lc]

    ti = lax.broadcasted_iota(I32, (lc, lc), 0)
    si = lax.broadcasted_iota(I32, (lc, lc), 1)
    causal = si <= ti

    def chunk_body(c, carry):
        r0 = pl.multiple_of(c * lc, lc)
        rows = pl.ds(r0, lc)
        gcol = gcol_ref[rows, :]
        grow = grow_ref[c]
        nb = n_ref[...].astype(BF16)
        for h in range(heads):
            cols = slice(h * hd, (h + 1) * hd)
            qc = q_ref[rows, cols]
            kc = k_ref[rows, cols]
            vc = v_ref[rows, cols]
            ig_c = gcol[:, h:h + 1]
            ig_r = grow[h:h + 1, :]
            lf_c = gcol[:, heads + h:heads + h + 1]
            lf_r = grow[heads + h:heads + h + 1, :]
            bcum_c = jnp.sum(jnp.where(causal, lf_r, 0.0), axis=1, keepdims=True)
            bcum_r = jnp.sum(jnp.where(ti <= si, lf_c, 0.0), axis=0, keepdims=True)
            m_st = m_ref[h:h + 1, 0:1]
            dm = jnp.where(causal, bcum_c - bcum_r + ig_r, -jnp.inf)
            inter = bcum_c + m_st
            m_t = jnp.maximum(inter, jnp.max(dm, axis=1, keepdims=True))
            dexp = jnp.exp(dm - m_t)
            sc = jnp.exp(inter - m_t)
            sco = (_dot_nt(qc, kc) * scale) * dexp
            cst = c_ref[h]
            num = _dot(sco.astype(BF16), vc) + sc * _dot(qc, cst.astype(BF16))
            qn = _dot_nt(qc, nb)[:, h:h + 1]
            den = jnp.sum(sco, axis=1, keepdims=True) + sc * qn
            hc = num * (1.0 / jnp.maximum(jnp.abs(den), jnp.exp(-m_t)))
            m_new = m_t[lc - 1:lc, :]
            b_last = bcum_c[lc - 1:lc, :]
            w_c = jnp.exp(b_last - bcum_c + ig_c - m_new) * scale
            w_r = jnp.exp(b_last - bcum_r + ig_r - m_new) * scale
            decay = jnp.exp(b_last + m_st - m_new)
            kw = kc * w_c.astype(BF16)
            c_ref[h] = decay * cst + _dot_tn(kw, vc)
            wk = _dot(jnp.broadcast_to(w_r, (SUBLANES, lc)).astype(BF16), kc)
            n_ref[h:h + 1, :] = decay * n_ref[h:h + 1, :] + wk[0:1, :]
            m_ref[h:h + 1, :] = jnp.broadcast_to(m_new, (1, LANES))
            hnrm = hc * lax.rsqrt(jnp.mean(hc * hc, axis=-1, keepdims=True) + RMS_EPS) * ng_ref[:, cols]
            zc = z_ref[rows, cols].astype(F32)
            xcc = xc_ref[rows, cols].astype(F32)
            y_ref[rows, cols] = (jax.nn.sigmoid(zc) * (hnrm + skip_ref[:, cols] * xcc)).astype(BF16)
        return carry

    lax.fori_loop(0, tm // lc, chunk_body, 0, unroll=True)
    o_ref[...] = x + _dot(y_ref[...], wout_ref[...])


def _mlstm_layer(x, g, w_in, conv_w, conv_b, w_q, w_k, w_v, w_gate, b_gate, norm_g, skip, w_out, heads,
                 layer):
    s, d = x.shape
    inner = w_out.shape[1]
    tm = min(TOKEN_TILE, s)
    ng = 2 * heads
    wqk = jnp.concatenate([_block_diag_tiles(w_q, MXU_DIM), _block_diag_tiles(w_k, MXU_DIM)],
                          axis=-1).astype(BF16)
    wv = _block_diag_tiles(w_v, MXU_DIM).astype(BF16)
    wgp = jnp.pad(w_gate, ((0, 0), (0, LANES - ng))).astype(BF16)
    wg_parts = [wgp[p * inner:(p + 1) * inner] for p in range(3)]
    bgp = jnp.pad(b_gate, (0, LANES - ng)).reshape(1, LANES)
    nqb = inner // MXU_DIM
    kern = functools.partial(_mlstm_kernel, tm=tm, inner=inner, heads=heads)
    in_specs = [_row_spec(tm, d), _const_spec((1, d)), _layer_spec(w_in, layer),
                _const_spec((CONV_W, inner)), _const_spec((1, inner)),
                _const_spec((nqb, MXU_DIM, 2 * MXU_DIM)), _const_spec((nqb, MXU_DIM, MXU_DIM))]
    in_specs += [_const_spec((inner, LANES))] * 3
    in_specs += [_const_spec((1, LANES)), _const_spec((1, inner)), _const_spec((1, inner)),
                 _layer_spec(w_out, layer)]
    hd = inner // heads
    scratch = [pltpu.VMEM((SUBLANES, inner), F32)]
    scratch += [pltpu.VMEM((tm, inner), BF16)] * 6
    scratch += [pltpu.VMEM((tm, LANES), F32), pltpu.VMEM((tm // ML_CHUNK, SUBLANES, ML_CHUNK), F32),
                pltpu.VMEM((heads, hd, hd), F32), pltpu.VMEM((SUBLANES, hd), F32),
                pltpu.VMEM((SUBLANES, LANES), F32)]
    return pl.pallas_call(
        kern,
        out_shape=jax.ShapeDtypeStruct((s, d), F32),
        grid=(s // tm,),
        in_specs=in_specs,
        out_specs=_row_spec(tm, d),
        scratch_shapes=scratch,
        compiler_params=_cparams(1, 56),
        name="mlstm_layer",
    )(x, g.reshape(1, d), w_in, conv_w, conv_b.reshape(1, inner), wqk, wv, *wg_parts, bgp,
      norm_g.reshape(1, inner), skip.reshape(1, inner), w_out)


def _s5_in_kernel(x_ref, g_ref, o_ref, slab_ref, *, tm):
    hn = _rms(x_ref[...], g_ref[...])
    nslab = hn.shape[1] // LANES
    for k in range(nslab):
        slab_ref[k] = hn[:, k * LANES:(k + 1) * LANES]
    for s in range(S5_CHUNK):
        for k in range(nslab):
            rows = slab_ref[k, pl.ds(s, tm // S5_CHUNK, stride=S5_CHUNK), :]
            o_ref[s, :, k * LANES:(k + 1) * LANES] = rows.astype(BF16)


def _s5_scan_kernel(xs_ref, t_ref, bc_ref, cc_ref, pre_ref, pim_ref, ys_ref,
                    sre_ref, sim_ref, cre_ref, cim_ref, *, cb, half):
    lc = S5_CHUNK

    @pl.when(pl.program_id(1) == 0)
    def _():
        cre_ref[...] = jnp.zeros_like(cre_ref)
        cim_ref[...] = jnp.zeros_like(cim_ref)

    u = jnp.concatenate([xs_ref[s] for s in range(lc)], axis=1)
    npair = (lc * LANES) // MXU_DIM
    ytiles = []
    for b in range(npair):
        acc = _dot(u[:, 0:MXU_DIM], t_ref[b])
        for a in range(1, b + 1):
            acc = acc + _dot(u[:, a * MXU_DIM:(a + 1) * MXU_DIM], t_ref[b - a])
        ytiles.append(acc)
    sinc = _dot(u, bc_ref[...])
    sre_ref[...] = sinc[:, :half]
    sim_ref[...] = sinc[:, half:]
    cin_re, cin_im = cre_ref[...], cim_ref[...]
    pr0, pi0 = pre_ref[0], pim_ref[0]
    sre_ref[0:1, :] = sre_ref[0:1, :] + (pr0 * cin_re - pi0 * cin_im)
    sim_ref[0:1, :] = sim_ref[0:1, :] + (pr0 * cin_im + pi0 * cin_re)
    xre, xim = sre_ref[...], sim_ref[...]
    row = lax.broadcasted_iota(I32, xre.shape, 0)
    d, kstep = 1, 0
    while d < cb:
        keep = row >= d
        re_sh = jnp.where(keep, pltpu.roll(xre, d, 0), 0.0)
        im_sh = jnp.where(keep, pltpu.roll(xim, d, 0), 0.0)
        pr, pi = pre_ref[kstep], pim_ref[kstep]
        xre, xim = xre + (pr * re_sh - pi * im_sh), xim + (pr * im_sh + pi * re_sh)
        d *= 2
        kstep += 1
    cre_ref[...] = xre[cb - 1:cb, :]
    cim_ref[...] = xim[cb - 1:cb, :]
    first = row == 0
    prev_re = jnp.where(first, cin_re, pltpu.roll(xre, 1, 0)).astype(BF16)
    prev_im = jnp.where(first, cin_im, pltpu.roll(xim, 1, 0)).astype(BF16)
    prev = jnp.concatenate([prev_re, prev_im], axis=1)
    for b in range(npair):
        yb = ytiles[b] + _dot(prev, cc_ref[:, b * MXU_DIM:(b + 1) * MXU_DIM])
        ys_ref[2 * b] = yb[:, :LANES]
        ys_ref[2 * b + 1] = yb[:, LANES:]


def _cmul(ar, ai, br, bi):
    return ar * br - ai * bi, ar * bi + ai * br


def _s5_operators(a_re, a_im, log_step, b_re, b_im, c_re, c_im, cb):
    lc = S5_CHUNK
    ng, np_, ni = b_re.shape
    gpv = LANES // ni
    nv = ng // gpv
    npair = lc // 2
    step = jnp.exp(log_step)[:, None]
    taus = jnp.arange(lc + 1, dtype=F32)[None, :, None]
    mag = jnp.exp((a_re * step)[:, None, :] * taus)
    ang = (a_im * step)[:, None, :] * taus
    pr, pi = mag * jnp.cos(ang), mag * jnp.sin(ang)
    lr, li = pr[:, 1], pi[:, 1]
    den = a_re * a_re + a_im * a_im
    fr, fi = _cmul(lr - 1.0, li, a_re / den, -a_im / den)
    bbr, bbi = _cmul(fr[..., None], fi[..., None], b_re, b_im)
    bbrt, bbit = bbr.transpose(0, 2, 1), bbi.transpose(0, 2, 1)
    mr, mi = _cmul(c_re[:, None], c_im[:, None], pr[:, :, None, :], pi[:, :, None, :])
    kk = jnp.sum(mr[:, :lc, None, :, :] * bbrt[:, None, :, None, :]
                 - mi[:, :lc, None, :, :] * bbit[:, None, :, None, :], axis=-1)
    grp16 = lambda r: (r // ni) % gpv
    lag_of = np.zeros((npair, 2, 2), np.int64)
    for dd in range(npair):
        for s2 in range(2):
            for t2 in range(2):
                lag_of[dd, s2, t2] = 2 * dd + t2 - s2
    sel = (np.arange(lc)[:, None, None, None] == lag_of[None]).astype(np.float32)
    sp_t = np.einsum('ldst,jk->ljdstk', sel, np.eye(ni, dtype=np.float32))
    sp_t = np.broadcast_to(sp_t[:, :, :, :, :, None, :], (lc, ni, npair, 2, 2, gpv, ni))
    sp_t = jnp.asarray(sp_t.reshape(lc * ni, npair * 2 * 2 * gpv * ni))
    kk2 = kk.transpose(0, 2, 1, 3).reshape(ng * ni, lc * ni)
    toep = _spread_masked(kk2, sp_t, grp16, grp16)
    toep = toep.reshape(nv, gpv * ni, npair * 2, 2 * gpv * ni).transpose(0, 2, 1, 3)
    toep = toep.reshape(nv, npair, MXU_DIM, MXU_DIM)
    rtaus = (lc - 1) - taus[:, :lc]
    rmag = jnp.exp((a_re * step)[:, None, :] * rtaus)
    rang = (a_im * step)[:, None, :] * rtaus
    rev_r, rev_i = rmag * jnp.cos(rang), rmag * jnp.sin(rang)
    bcr, bci = _cmul(rev_r[:, :, None, :], rev_i[:, :, None, :], bbrt[:, None], bbit[:, None])
    bcx = jnp.concatenate([bcr, bci], axis=-1).reshape(nv, gpv, lc, ni, 2 * np_).transpose(0, 2, 1, 3, 4)
    sp_b = jnp.tile(jnp.eye(2 * np_, dtype=F32).reshape(2 * np_, 2, 1, np_), (1, 1, gpv, 1)).reshape(2 * np_, -1)
    bc = _spread_masked(bcx.reshape(-1, 2 * np_), sp_b, grp16, lambda c: (c // np_) % gpv)
    bc = bc.reshape(nv, lc * gpv * ni, 2 * gpv * np_)
    ccx = jnp.stack([mr[:, 1:lc + 1], -mi[:, 1:lc + 1]], axis=1)
    ccx = ccx.reshape(nv, gpv, 2, lc, ni, np_).transpose(0, 2, 1, 5, 3, 4)
    sp_c = jnp.tile(jnp.eye(lc * ni, dtype=F32).reshape(lc * ni, lc, 1, ni), (1, 1, gpv, 1)).reshape(lc * ni, -1)
    cc = _spread_masked(ccx.reshape(-1, lc * ni), sp_c, lambda r: (r // np_) % gpv, grp16)
    cc = cc.reshape(nv, 2 * gpv * np_, lc * gpv * ni)
    nsteps = max(1, int(math.ceil(math.log2(max(cb, 2)))))
    sq = jnp.stack([pr[:, lc].reshape(1, -1), pi[:, lc].reshape(1, -1)])
    scan = []
    for _ in range(nsteps):
        scan.append(sq)
        sq = jnp.stack(_cmul(sq[0], sq[1], sq[0], sq[1]))
    scan = jnp.stack(scan, axis=1)
    return toep, bc, cc, scan[0], scan[1]


def _s5_out_kernel(x_ref, ys_ref, g_ref, d_ref, wglu_ref, bglu_ref, o_ref, slab_ref, *, tm):
    x = x_ref[...]
    nslab = x.shape[1] // LANES
    for s in range(S5_CHUNK):
        for k in range(nslab):
            slab_ref[k, pl.ds(s, tm // S5_CHUNK, stride=S5_CHUNK), :] = ys_ref[s, :, k * LANES:(k + 1) * LANES]
    yt = jnp.concatenate([slab_ref[k] for k in range(nslab)], axis=1)
    hn = _rms(x, g_ref[...])
    y = _gelu_tanh(yt + d_ref[...] * hn)
    o_ref[...] = x + y * jax.nn.sigmoid(_dot(y.astype(BF16), wglu_ref[...]) + bglu_ref[...])


def _s5_layer(x, g, a_re, a_im, log_step, b_re, b_im, c_re, c_im, d_skip, w_glu, b_glu, layer):
    s, d = x.shape
    ngroups, pstate, gch = b_re.shape
    lc = S5_CHUNK
    nchunk = s // lc
    cb = min(S5_CHUNK_BLOCK, nchunk)
    gpv = LANES // gch
    nv = ngroups // gpv
    half = gpv * pstate
    toep, bc, cc, scan_r, scan_i = _s5_operators(a_re, a_im, log_step, b_re, b_im, c_re, c_im, cb)
    nsteps = scan_r.shape[0]
    tm = min(TOKEN_TILE, s)
    nslab = d // LANES
    step_major = pl.BlockSpec((lc, tm // lc, d), lambda i: (0, i, 0))
    xs = pl.pallas_call(
        functools.partial(_s5_in_kernel, tm=tm),
        out_shape=jax.ShapeDtypeStruct((lc, nchunk, d), BF16),
        grid=(s // tm,),
        in_specs=[_row_spec(tm, d), _const_spec((1, d))],
        out_specs=step_major,
        scratch_shapes=[pltpu.VMEM((nslab, tm, LANES), F32)],
        compiler_params=_cparams(1, 24),
        name="s5_in",
    )(x, g.reshape(1, d))
    wl = lc * LANES
    npair = lc // 2
    ys = pl.pallas_call(
        functools.partial(_s5_scan_kernel, cb=cb, half=half),
        out_shape=jax.ShapeDtypeStruct((lc, nchunk, d), F32),
        grid=(nv, nchunk // cb),
        in_specs=[pl.BlockSpec((lc, cb, LANES), lambda v, j: (0, j, v)),
                  pl.BlockSpec((None, npair, MXU_DIM, MXU_DIM), lambda v, j: (v, 0, 0, 0)),
                  pl.BlockSpec((None, wl, 2 * half), lambda v, j: (v, 0, 0)),
                  pl.BlockSpec((None, 2 * half, wl), lambda v, j: (v, 0, 0)),
                  pl.BlockSpec((nsteps, 1, half), lambda v, j: (0, 0, v)),
                  pl.BlockSpec((nsteps, 1, half), lambda v, j: (0, 0, v))],
        out_specs=pl.BlockSpec((lc, cb, LANES), lambda v, j: (0, j, v)),
        scratch_shapes=[pltpu.VMEM((cb, half), F32), pltpu.VMEM((cb, half), F32),
                        pltpu.VMEM((1, half), F32), pltpu.VMEM((1, half), F32)],
        compiler_params=_cparams(2, 52),
        name="s5_scan",
    )(xs, toep, bc, cc, scan_r, scan_i)
    return pl.pallas_call(
        functools.partial(_s5_out_kernel, tm=tm),
        out_shape=jax.ShapeDtypeStruct((s, d), F32),
        grid=(s // tm,),
        in_specs=[_row_spec(tm, d), step_major, _const_spec((1, d)), _const_spec((1, d)),
                  _layer_spec(w_glu, layer), _const_spec((1, d))],
        out_specs=_row_spec(tm, d),
        scratch_shapes=[pltpu.VMEM((nslab, tm, LANES), F32)],
        compiler_params=_cparams(1, 32),
        name="s5_out",
    )(x, ys, g.reshape(1, d), d_skip.reshape(1, d), w_glu, b_glu.reshape(1, d))


def _router_kernel(x_ref, g_ref, wr_ref, br_ref, hn_ref, wts_ref, idx_ref, cnt_ref, run_ref, *, tm, ne):
    @pl.when(pl.program_id(0) == 0)
    def _():
        run_ref[...] = jnp.zeros_like(run_ref)

    hn = _rms(x_ref[...], g_ref[...])
    for k in range(hn.shape[1] // LANES):
        hn_ref[pl.ds(k, tm, stride=SUBLANES), :] = hn[:, k * LANES:(k + 1) * LANES]
    wr = wr_ref[...]
    hn_hi, wr_hi = hn.astype(BF16), wr.astype(BF16)
    hn_lo, wr_lo = (hn - hn_hi.astype(F32)).astype(BF16), (wr - wr_hi.astype(F32)).astype(BF16)
    logits = (_dot(hn_hi, wr_hi) + _dot(hn_hi, wr_lo)) + (_dot(hn_lo, wr_hi) + _dot(hn_lo, wr_lo))
    logits = logits + br_ref[...]
    lane = lax.broadcasted_iota(I32, (tm, ne), 1).astype(F32)
    m1 = jnp.max(logits, axis=1, keepdims=True)
    i1 = jnp.min(jnp.where(logits == m1, lane, float(ne)), axis=1, keepdims=True)
    sel1 = lane == i1
    rest = jnp.where(sel1, -jnp.inf, logits)
    m2 = jnp.max(rest, axis=1, keepdims=True)
    i2 = jnp.min(jnp.where(rest == m2, lane, float(ne)), axis=1, keepdims=True)
    sel2 = lane == i2
    e = jnp.exp(m2 - m1)
    w1 = 1.0 / (1.0 + e)
    w2 = e / (1.0 + e)
    sel = jnp.logical_or(sel1, sel2)
    ti = lax.broadcasted_iota(I32, (tm, tm), 0)
    si = lax.broadcasted_iota(I32, (tm, tm), 1)
    earlier = jnp.where(si < ti, 1.0, 0.0).astype(BF16)
    excl = _dot(earlier, jnp.where(sel, 1.0, 0.0).astype(BF16)) + run_ref[...]
    r1 = jnp.sum(jnp.where(sel1, excl, 0.0), axis=1, keepdims=True)
    r2 = jnp.sum(jnp.where(sel2, excl, 0.0), axis=1, keepdims=True)
    total = run_ref[...] + jnp.sum(jnp.where(sel, 1.0, 0.0), axis=0, keepdims=True)
    run_ref[...] = total
    cnt_ref[...] = total
    wts_ref[...] = jnp.where(lane == 0.0, w1, jnp.where(lane == 1.0, w2, 0.0))
    lane_w = lax.broadcasted_iota(I32, (tm, LANES), 1)
    meta = jnp.zeros((tm, LANES), F32)
    for c, val in enumerate([i1, i2, r1, r2]):
        meta = jnp.where(lane_w == c, val, meta)
    idx_ref[...] = meta.T[0:SUBLANES, :].astype(I32)


def _token_rows(t):
    return pl.ds(pl.multiple_of(t * SUBLANES, SUBLANES), SUBLANES)


def _untile_tokens(ref, n, nslab, lead=()):
    return jnp.concatenate([ref[lead + (pl.ds(k, n, stride=SUBLANES), slice(None))] for k in range(nslab)], axis=1)


def _dispatch_kernel(ends_ref, pos_ref, hn_ref, xs_ref, ring_ref, zero_ref, in_sem, out_sem, zsem, *, tg, tr, ne):
    i = pl.program_id(0)
    n = pl.num_programs(0)
    nrows = tg * SUBLANES

    def fetch(step, sl):
        src = hn_ref.at[pl.ds(pl.multiple_of(step * nrows, nrows), nrows), :]
        return pltpu.make_async_copy(src, ring_ref.at[sl], in_sem.at[sl])

    def row_copy(t, dst_row, sl):
        return pltpu.make_async_copy(ring_ref.at[sl, _token_rows(t), :], xs_ref.at[_token_rows(dst_row), :],
                                     out_sem.at[sl])

    def drain(sl):
        def body(t, carry):
            for k in range(TOP_K):
                row_copy(t, 0, sl).wait()
            return carry
        lax.fori_loop(0, tg, body, 0, unroll=8)

    @pl.when(i == 0)
    def _():
        fetch(0, 0).start()
        zero_ref[...] = jnp.zeros_like(zero_ref)
        last_tile = xs_ref.shape[0] // SUBLANES - tr
        for e in range(ne):
            for r0 in (jnp.maximum(ends_ref[e] - tr, 0), jnp.minimum(ends_ref[ne - 1] + e * tr, last_tile)):
                dst = xs_ref.at[pl.ds(pl.multiple_of(r0 * SUBLANES, tr * SUBLANES), tr * SUBLANES), :]
                cp = pltpu.make_async_copy(zero_ref, dst, zsem)
                cp.start()
                cp.wait()

    slot = i % 3
    fetch(i, slot).wait()

    @pl.when(i >= 2)
    def _():
        drain((i + 1) % 3)

    @pl.when(i + 1 < n)
    def _():
        fetch(i + 1, (i + 1) % 3).start()

    def issue(t, carry):
        for k in range(TOP_K):
            row_copy(t, pos_ref[k, t], slot).start(priority=k)
        return carry

    lax.fori_loop(0, tg, issue, 0, unroll=8)

    @pl.when(i == n - 1)
    def _():
        @pl.when(i >= 1)
        def _():
            drain((i + 2) % 3)
        drain(slot)


def _expert_kernel(te_ref, nv_ref, xs_ref, w1_ref, w3_ref, w2_ref, ys_ref, *, tr, ff_chunk):
    i = pl.program_id(0)
    nslab = w2_ref.shape[1] // LANES

    @pl.when(i < nv_ref[0])
    def _():
        xb = _untile_tokens(xs_ref, tr, nslab).astype(BF16)
        y = None
        for c in range(w1_ref.shape[1] // ff_chunk):
            sl = slice(c * ff_chunk, (c + 1) * ff_chunk)
            h = (_silu(_dot(xb, w1_ref[:, sl])) * _dot(xb, w3_ref[:, sl])).astype(BF16)
            part = _dot(h, w2_ref[sl, :])
            y = part if y is None else y + part
        for k in range(nslab):
            ys_ref[pl.ds(k, tr, stride=SUBLANES), :] = y[:, k * LANES:(k + 1) * LANES]

    @pl.when(i >= nv_ref[0])
    def _():
        ys_ref[...] = jnp.zeros_like(ys_ref)


def _combine_kernel(pos_ref, nxt_ref, x_ref, wts_ref, gf_ref, ys_ref, o_ref, buf_ref, sem, *, tg, final_norm):
    i = pl.program_id(0)
    slot = i % 2

    def row_copy(t, k, src_row, sl):
        return pltpu.make_async_copy(ys_ref.at[_token_rows(src_row), :], buf_ref.at[sl, k, _token_rows(t), :],
                                     sem.at[sl])

    def request(rows_ref, sl):
        def issue(t, carry):
            for k in range(TOP_K):
                row_copy(t, k, rows_ref[k, t], sl).start(priority=k)
            return carry
        lax.fori_loop(0, tg, issue, 0, unroll=8)

    @pl.when(i == 0)
    def _():
        request(pos_ref, 0)

    @pl.when(i + 1 < pl.num_programs(0))
    def _():
        request(nxt_ref, 1 - slot)

    def drain(t, carry):
        for k in range(TOP_K):
            row_copy(t, k, 0, slot).wait()
        return carry

    lax.fori_loop(0, tg, drain, 0, unroll=8)

    x = x_ref[...]
    nslab = x.shape[1] // LANES
    wts = wts_ref[...]
    out = x + (wts[:, 0:1] * _untile_tokens(buf_ref, tg, nslab, (slot, 0))
               + wts[:, 1:2] * _untile_tokens(buf_ref, tg, nslab, (slot, 1)))
    if final_norm:
        out = _rms(out, gf_ref[...])
    o_ref[...] = out


def _moe_layer(x, g, router, router_b, w1, w3, w2, layer, g_final, final_norm):
    s, d = x.shape
    ne = router.shape[1]
    f = w1.shape[3]
    tm = min(TOKEN_TILE, s)
    tg = min(GATHER_TILE, s)
    tr = GROUP_TILE
    sub = SUBLANES
    hn, wts, idx, cnt = pl.pallas_call(
        functools.partial(_router_kernel, tm=tm, ne=ne),
        out_shape=(jax.ShapeDtypeStruct((s * sub, LANES), F32), jax.ShapeDtypeStruct((s, ne), F32),
                   jax.ShapeDtypeStruct((sub, s), I32), jax.ShapeDtypeStruct((1, ne), F32)),
        grid=(s // tm,),
        in_specs=[_row_spec(tm, d), _const_spec((1, d)), _const_spec((d, ne)), _const_spec((1, ne))],
        out_specs=(_row_spec(tm * sub, LANES), _row_spec(tm, ne), pl.BlockSpec((sub, tm), lambda i: (0, i)),
                   pl.BlockSpec((1, ne), lambda i: (0, 0))),
        scratch_shapes=[pltpu.VMEM((1, ne), F32)],
        compiler_params=_cparams(1, 24),
        name="moe_router",
    )(x, g.reshape(1, d), router, router_b.reshape(1, ne))

    counts = cnt[0].astype(I32)
    padded = ((counts + tr - 1) // tr) * tr
    ends = jnp.cumsum(padded)
    starts = ends - padded
    n_tiles = (TOP_K * s) // tr + ne
    rows = n_tiles * tr
    n_valid = ends[-1] // tr
    tile_row = jnp.minimum(jnp.arange(n_tiles, dtype=I32), n_valid - 1) * tr
    tile_expert = jnp.minimum(jnp.sum((ends[None, :] <= tile_row[:, None]).astype(I32), axis=1), ne - 1)

    expert_start = jnp.sum(jnp.where(idx[0:TOP_K, :, None] == jnp.arange(ne, dtype=I32), starts, 0), axis=-1)
    pos = expert_start + idx[TOP_K:2 * TOP_K]

    pos_spec = pl.BlockSpec((TOP_K, tg), lambda i, *_: (0, i), memory_space=pltpu.SMEM)
    any_spec = pl.BlockSpec(memory_space=pl.ANY)
    xs = pl.pallas_call(
        functools.partial(_dispatch_kernel, tg=tg, tr=tr, ne=ne),
        out_shape=jax.ShapeDtypeStruct((rows * sub, LANES), F32),
        grid_spec=pltpu.PrefetchScalarGridSpec(
            num_scalar_prefetch=1,
            grid=(s // tg,),
            in_specs=[pos_spec, any_spec],
            out_specs=any_spec,
            scratch_shapes=[pltpu.VMEM((3, tg * sub, LANES), F32), pltpu.VMEM((tr * sub, LANES), F32),
                            pltpu.SemaphoreType.DMA((3,)), pltpu.SemaphoreType.DMA((3,)),
                            pltpu.SemaphoreType.DMA],
        ),
        compiler_params=_cparams(1, 16),
        name="moe_dispatch",
    )(ends, pos, hn)

    row_tile = lambda i, te, nv: (jnp.maximum(jnp.minimum(i, nv[0] - 1), 0), 0)
    expert_w = lambda i, te, nv: (layer, te[i], 0, 0)
    ys = pl.pallas_call(
        functools.partial(_expert_kernel, tr=tr, ff_chunk=MXU_DIM),
        out_shape=jax.ShapeDtypeStruct((rows * sub, LANES), F32),
        grid_spec=pltpu.PrefetchScalarGridSpec(
            num_scalar_prefetch=2,
            grid=(n_tiles,),
            in_specs=[pl.BlockSpec((tr * sub, LANES), row_tile),
                      pl.BlockSpec((None, None, d, f), expert_w),
                      pl.BlockSpec((None, None, d, f), expert_w),
                      pl.BlockSpec((None, None, f, d), expert_w)],
            out_specs=pl.BlockSpec((tr * sub, LANES), lambda i, te, nv: (i, 0)),
        ),
        compiler_params=_cparams(1, 54),
        name="moe_experts",
    )(tile_expert, n_valid.reshape(1), xs, w1, w3, w2)

    return pl.pallas_call(
        functools.partial(_combine_kernel, tg=tg, final_norm=final_norm),
        out_shape=jax.ShapeDtypeStruct((s, d), F32),
        grid=(s // tg,),
        in_specs=[pl.BlockSpec((TOP_K, tg), lambda i: (0, i), memory_space=pltpu.SMEM),
                  pl.BlockSpec((TOP_K, tg), lambda i: (0, jnp.minimum(i + 1, s // tg - 1)),
                               memory_space=pltpu.SMEM),
                  _row_spec(tg, d), _row_spec(tg, ne), _const_spec((1, d)), any_spec],
        out_specs=_row_spec(tg, d),
        scratch_shapes=[pltpu.VMEM((2, TOP_K, tg * sub, LANES), F32), pltpu.SemaphoreType.DMA((2,))],
        compiler_params=_cparams(1, 24),
        name="moe_combine",
    )(pos, pos, x, wts, g_final.reshape(1, d), ys)


def _norm_kernel(x_ref, g_ref, o_ref):
    o_ref[...] = _rms(x_ref[...], g_ref[...])


def _final_norm(x, g):
    s, d = x.shape
    tm = min(TOKEN_TILE, s)
    return pl.pallas_call(
        _norm_kernel,
        out_shape=jax.ShapeDtypeStruct((s, d), F32),
        grid=(s // tm,),
        in_specs=[_row_spec(tm, d), _const_spec((1, d))],
        out_specs=_row_spec(tm, d),
        compiler_params=_cparams(1, 16),
        name="final_norm",
    )(x, g.reshape(1, d))


def kernel(x, norm_mix, norm_ffn, norm_final, a_w_in, a_conv_w, a_conv_b, a_w_r, a_b_r, a_w_i, a_b_i, a_lam, a_w_out, b_w_in, b_conv_w, b_conv_b, b_w_q, b_w_k, b_w_v, b_w_gate, b_b_gate, b_norm, b_skip, b_w_out, c_a_re, c_a_im, c_log_step, c_b_re, c_b_im, c_c_re, c_c_im, c_d, c_w_glu, c_b_glu, f_w1, f_w3, f_w2, e_router, e_router_b, e_w1, e_w3, e_w2):
    bsz, seq, d = x.shape
    depth = norm_mix.shape[0]
    heads = b_b_gate.shape[1] // 2
    a_w_in, a_w_out, b_w_in, b_w_out, c_w_glu = (w.astype(BF16) for w in (a_w_in, a_w_out, b_w_in, b_w_out,
                                                                            c_w_glu))
    f_w1, f_w3, f_w2 = (w.astype(BF16) for w in (f_w1, f_w3, f_w2))
    expert_f32 = {"w1": e_w1, "w3": e_w3, "w2": e_w2}
    expert_bf16 = {}
    steps = seq // min(TOKEN_TILE, seq)

    def can_ride(name):
        w = expert_f32[name]
        rows = w.shape[1] * w.shape[2]
        blk = rows // steps
        return rows % steps == 0 and blk % (2 * SUBLANES) == 0 and blk * w.shape[3] * 4 <= 4 * 1024 * 1024

    def pending(names, layer_i):
        moe_layers = [m for m in range(layer_i, depth) if m % 2 == 1]
        if not moe_layers:
            return [], []
        fx = moe_layers[0] // 2
        keys = [(nm, fx) for nm in names if (nm, fx) not in expert_bf16 and can_ride(nm)]
        jobs = [(expert_f32[nm].reshape(-1, expert_f32[nm].shape[3]),
                 expert_f32[nm].shape[1] * expert_f32[nm].shape[2], fx) for nm, _ in keys]
        return keys, jobs

    def record(keys, casts):
        for (nm, fx), c in zip(keys, casts):
            expert_bf16[(nm, fx)] = c.reshape((1,) + expert_f32[nm].shape[1:])

    outs = []
    for bi in range(bsz):
        h = x[bi]
        for i in range(depth):
            kind, j = i % 3, i // 3
            if kind == 0:
                keys, jobs = pending(["w2"], i)
                h, casts = _rglru_layer(h, norm_mix[i], a_w_in, a_conv_w[j], a_conv_b[j], a_w_r[j], a_b_r[j],
                                        a_w_i[j], a_b_i[j], a_lam[j], a_w_out, j, jobs)
                record(keys, casts)
            elif kind == 1:
                h = _mlstm_layer(h, norm_mix[i], b_w_in, b_conv_w[j], b_conv_b[j], b_w_q[j], b_w_k[j],
                                 b_w_v[j], b_w_gate[j], b_b_gate[j], b_norm[j], b_skip[j], b_w_out, heads, j)
            else:
                h = _s5_layer(h, norm_mix[i], c_a_re[j], c_a_im[j], c_log_step[j], c_b_re[j], c_b_im[j],
                              c_c_re[j], c_c_im[j], c_d[j], c_w_glu, c_b_glu[j], j)
            fidx = i // 2
            last = i == depth - 1
            if i % 2 == 0:
                keys, jobs = pending(["w1", "w3"], i + 1)
                h, casts = _ffn_dense(h, norm_ffn[i], f_w1, f_w3, f_w2, fidx, jobs)
                record(keys, casts)
                if last:
                    h = _final_norm(h, norm_final)
            else:
                ew = [expert_bf16[(nm, fidx)] if (nm, fidx) in expert_bf16
                      else expert_f32[nm][fidx:fidx + 1].astype(BF16) for nm in ("w1", "w3", "w2")]
                h = _moe_layer(h, norm_ffn[i], e_router[fidx], e_router_b[fidx], ew[0], ew[1], ew[2], 0,
                               norm_final, last)
        outs.append(h)
    return jnp.stack(outs)
```

```python
import functools
import math
from typing import NamedTuple

import jax
import jax.numpy as jnp
import numpy as np
from jax import lax
from jax.experimental import pallas as pl
from jax.experimental.pallas import tpu as pltpu

F32 = jnp.float32
BF16 = jnp.bfloat16
I32 = jnp.int32

RMS_EPS = 1e-6
CONV_W = 4
RG_C = 8.0
ML_CHUNK = 128
TOP_K = 2

V7X_VMEM_BYTES = 64 * 1024 * 1024
SUBLANES = 8
LANES = 128
MXU_DIM = 256

TOKEN_TILE = 512
GATHER_TILE = 512
GROUP_TILE = 512
S5_CHUNK = 16
S5_CHUNK_BLOCK = 512


def _cparams(n_axes, vmem_mib):
    return pltpu.CompilerParams(
        dimension_semantics=("arbitrary",) * n_axes,
        vmem_limit_bytes=min(vmem_mib * 1024 * 1024, V7X_VMEM_BYTES - 6 * 1024 * 1024),
    )


def _const_spec(shape):
    nd = len(shape)
    return pl.BlockSpec(shape, lambda *_: (0,) * nd, pipeline_mode=pl.Buffered(1))


class _LayerWeight(NamedTuple):
    array: jax.Array
    index: int


def _layer_spec(w):
    nd = w.array.ndim
    return pl.BlockSpec((None,) + tuple(w.array.shape[1:]), lambda *_: (w.index,) + (0,) * (nd - 1),
                        pipeline_mode=pl.Buffered(1))


def _row_spec(tm, d):
    return pl.BlockSpec((tm, d), lambda i: (i, 0))


def _rms(x, g):
    return x * lax.rsqrt(jnp.mean(x * x, axis=-1, keepdims=True) + RMS_EPS) * g


def _dot(a, b):
    return jnp.dot(a, b, preferred_element_type=F32)


def _dot_nt(a, b):
    return lax.dot_general(a, b, (((1,), (1,)), ((), ())), preferred_element_type=F32)


def _dot_tn(a, b):
    return lax.dot_general(a, b, (((0,), (0,)), ((), ())), preferred_element_type=F32)


def _silu(x):
    return x * jax.nn.sigmoid(x)


def _gelu_tanh(x):
    c = math.sqrt(2.0 / math.pi)
    half = 0.5 * x
    return half + half * jnp.tanh(x * (c + (0.044715 * c) * (x * x)))


def _log_sigmoid(x):
    return jnp.minimum(x, 0.0) - jnp.log1p(jnp.exp(-jnp.abs(x)))


def _causal_conv(tail_ref, xin, cw, cb, cols, tm):
    prev = tail_ref[:, cols]
    sub = lax.broadcasted_iota(I32, prev.shape, 0)
    out = cb
    for d in (3, 2, 1):
        sh = pltpu.roll(xin, d, 0)
        head = jnp.where(sub < d, pltpu.roll(prev, d, 0), sh[0:SUBLANES])
        out = out + cw[3 - d:4 - d] * jnp.concatenate([head, sh[SUBLANES:]], axis=0)
    out = out + cw[3:4] * xin
    tail_ref[:, cols] = xin[tm - SUBLANES:tm, :]
    return out


def _spread_masked(x, spread, row_group, col_group):
    y = jnp.dot(x.astype(BF16), spread.astype(BF16), preferred_element_type=F32)
    keep = row_group(jnp.arange(x.shape[0]))[:, None] == col_group(jnp.arange(spread.shape[1]))[None, :]
    return jnp.where(keep, y, 0.0).astype(BF16)


def _block_diag_tiles(w, tile):
    nb, k, _ = w.shape
    per = tile // k
    spread = jnp.tile(jnp.eye(k, dtype=F32), (1, per))
    tiles = _spread_masked(w.reshape(nb * k, k), spread, lambda r: (r // k) % per, lambda c: c // k)
    return tiles.reshape(nb // per, tile, tile)


def _cast_plan(jobs, steps):
    in_specs, out_specs, out_shapes, args = [], [], [], []
    for arr, layer_rows, layer in jobs:
        blk = layer_rows // steps
        cols = arr.shape[1]
        base = layer * steps
        in_specs.append(pl.BlockSpec((blk, cols), lambda i, base=base: (base + i, 0)))
        out_specs.append(pl.BlockSpec((blk, cols), lambda i: (i, 0)))
        out_shapes.append(jax.ShapeDtypeStruct((layer_rows, cols), BF16))
        args.append(arr)
    return in_specs, out_specs, out_shapes, args


def _run_casts(cast_refs):
    n = len(cast_refs) // 2
    for src, dst in zip(cast_refs[:n], cast_refs[n:]):
        dst[...] = src[...].astype(BF16)


def _ffn_dense_kernel(x_ref, g_ref, w1_ref, w3_ref, w2_ref, *rest, ff_chunk, ncast):
    o_ref = rest[ncast]
    x = x_ref[...]
    hn = _rms(x, g_ref[...]).astype(BF16)
    acc = x
    for c in range(w1_ref.shape[1] // ff_chunk):
        sl = slice(c * ff_chunk, (c + 1) * ff_chunk)
        h = (_silu(_dot(hn, w1_ref[:, sl])) * _dot(hn, w3_ref[:, sl])).astype(BF16)
        acc = acc + _dot(h, w2_ref[sl, :])
    o_ref[...] = acc
    _run_casts(rest[:ncast] + rest[ncast + 1:])


def _ffn_dense(x, g, w1, w3, w2, cast_jobs=()):
    s, d = x.shape
    tm = min(TOKEN_TILE, s)
    steps = s // tm
    c_in, c_out, c_shapes, c_args = _cast_plan(cast_jobs, steps)
    outs = pl.pallas_call(
        functools.partial(_ffn_dense_kernel, ff_chunk=MXU_DIM, ncast=len(cast_jobs)),
        out_shape=[jax.ShapeDtypeStruct((s, d), F32)] + c_shapes,
        grid=(steps,),
        in_specs=[_row_spec(tm, d), _const_spec((1, d)), _layer_spec(w1), _layer_spec(w3), _layer_spec(w2)] + c_in,
        out_specs=[_row_spec(tm, d)] + c_out,
        compiler_params=_cparams(1, 56),
        name="ffn_dense",
    )(x, g.reshape(1, d), w1.array, w3.array, w2.array, *c_args)
    return outs[0], list(outs[1:])


def _linear_scan_rows(a, b, h0):
    tm, c = a.shape
    ngrp = tm // SUBLANES
    a3 = a.reshape(ngrp, SUBLANES, c)
    b3 = b.reshape(ngrp, SUBLANES, c)
    sub = lax.broadcasted_iota(I32, (1, SUBLANES, c), 1)
    for d in (1, 2, 4):
        keep = sub >= d
        a_sh = jnp.where(keep, pltpu.roll(a3, d, 1), 1.0)
        b_sh = jnp.where(keep, pltpu.roll(b3, d, 1), 0.0)
        b3 = a3 * b_sh + b3
        a3 = a3 * a_sh
    carry = h0
    rows = []
    for j in range(ngrp):
        hj = b3[j] + a3[j] * carry
        rows.append(hj)
        carry = hj[SUBLANES - 1:SUBLANES, :]
    return jnp.concatenate(rows, axis=0), carry


def _rglru_kernel(x_ref, g_ref, win_ref, cw_ref, cb_ref, wg_ref, bg_ref, nsp_ref, wout_ref, *rest,
                  tm, width, ncast):
    o_ref = rest[ncast]
    tail_ref, h_ref = rest[2 * ncast + 1:]
    bd = MXU_DIM

    @pl.when(pl.program_id(0) == 0)
    def _():
        tail_ref[...] = jnp.zeros_like(tail_ref)
        h_ref[...] = jnp.zeros_like(h_ref)

    x = x_ref[...]
    hn = _rms(x, g_ref[...]).astype(BF16)
    ys = []
    for n in range(width // bd):
        cols = slice(n * bd, (n + 1) * bd)
        gx = _dot(hn, win_ref[:, cols])
        xr = _dot(hn, win_ref[:, width + n * bd:width + (n + 1) * bd])
        xr = _causal_conv(tail_ref, xr, cw_ref[:, cols], cb_ref[:, cols], cols, tm)
        gates = _dot(xr.astype(BF16), wg_ref[n]) + bg_ref[n]
        r = jax.nn.sigmoid(gates[:, :bd])
        ig = jax.nn.sigmoid(gates[:, bd:])
        a = jnp.exp(r * nsp_ref[:, cols])
        v = 1.0 - a * a
        b = jnp.where(v > 0.0, v * lax.rsqrt(v), 0.0) * (ig * xr)
        h, last = _linear_scan_rows(a, b, h_ref[:, cols])
        h_ref[:, cols] = last
        ys.append((_gelu_tanh(gx) * h).astype(BF16))
    y = jnp.concatenate(ys, axis=1)
    o_ref[...] = x + _dot(y, wout_ref[...])
    _run_casts(rest[:ncast] + rest[ncast + 1:2 * ncast + 1])


def _rglru_layer(x, g, w_in, conv_w, conv_b, w_r, b_r, w_i, b_i, lam, w_out, cast_jobs=()):
    s, d = x.shape
    width = w_out.array.shape[1]
    bd = MXU_DIM
    nblk = width // bd
    tm = min(TOKEN_TILE, s)
    wg = jnp.concatenate([_block_diag_tiles(w_r, bd), _block_diag_tiles(w_i, bd)], axis=-1).astype(BF16)
    bg = jnp.concatenate([b_r.reshape(nblk, 1, bd), b_i.reshape(nblk, 1, bd)], axis=-1)
    nsp = (-RG_C * jax.nn.softplus(-lam)).reshape(1, width)
    steps = s // tm
    c_in, c_out, c_shapes, c_args = _cast_plan(cast_jobs, steps)
    kern = functools.partial(_rglru_kernel, tm=tm, width=width, ncast=len(cast_jobs))
    outs = pl.pallas_call(
        kern,
        out_shape=[jax.ShapeDtypeStruct((s, d), F32)] + c_shapes,
        grid=(steps,),
        in_specs=[_row_spec(tm, d), _const_spec((1, d)), _layer_spec(w_in),
                  _const_spec((CONV_W, width)), _const_spec((1, width)),
                  _const_spec((nblk, bd, 2 * bd)), _const_spec((nblk, 1, 2 * bd)),
                  _const_spec((1, width)), _layer_spec(w_out)] + c_in,
        out_specs=[_row_spec(tm, d)] + c_out,
        scratch_shapes=[pltpu.VMEM((SUBLANES, width), F32), pltpu.VMEM((1, width), F32)],
        compiler_params=_cparams(1, 48),
        name="rglru_layer",
    )(x, g.reshape(1, d), w_in.array, conv_w, conv_b.reshape(1, width), wg, bg, nsp, w_out.array, *c_args)
    return outs[0], list(outs[1:])


def _mlstm_kernel(x_ref, g_ref, win_ref, cw_ref, cb_ref, wqk_ref, wv_ref, wgq_ref, wgk_ref, wgv_ref,
                  bg_ref, ng_ref, skip_ref, wout_ref, o_ref,
                  tail_ref, q_ref, k_ref, v_ref, xc_ref, z_ref, y_ref, gcol_ref, grow_ref,
                  c_ref, n_ref, m_ref, *, tm, inner, heads):
    hd = inner // heads
    lc = ML_CHUNK
    nqb = inner // MXU_DIM
    scale = hd ** -0.5

    @pl.when(pl.program_id(0) == 0)
    def _():
        tail_ref[...] = jnp.zeros_like(tail_ref)
        c_ref[...] = jnp.zeros_like(c_ref)
        n_ref[...] = jnp.zeros_like(n_ref)
        m_ref[...] = jnp.zeros_like(m_ref)

    x = x_ref[...]
    hn = _rms(x, g_ref[...]).astype(BF16)

    for b in range(nqb):
        cols = slice(b * MXU_DIM, (b + 1) * MXU_DIM)
        xm = _dot(hn, win_ref[:, cols])
        xc = _silu(_causal_conv(tail_ref, xm, cw_ref[:, cols], cb_ref[:, cols], cols, tm))
        xcb = xc.astype(BF16)
        qk = _dot(xcb, wqk_ref[b])
        q_ref[:, cols] = qk[:, :MXU_DIM].astype(BF16)
        k_ref[:, cols] = qk[:, MXU_DIM:].astype(BF16)
        v_ref[:, cols] = _dot(xm.astype(BF16), wv_ref[b]).astype(BF16)
        xc_ref[:, cols] = xcb
        z_ref[:, cols] = _dot(hn, win_ref[:, inner + b * MXU_DIM:inner + (b + 1) * MXU_DIM]).astype(BF16)

    gc = (_dot(q_ref[...], wgq_ref[...]) + _dot(k_ref[...], wgk_ref[...]) + _dot(v_ref[...], wgv_ref[...])
          + bg_ref[...])
    lane = lax.broadcasted_iota(I32, gc.shape, 1)
    gc = jnp.where(lane < heads, gc, _log_sigmoid(gc))
    gcol_ref[...] = gc
    gr = gc.T
    for c in range(tm // lc):
        grow_ref[c] = gr[0:SUBLANES, c * lc:(c + 1) * lc]

    ti = lax.broadcasted_iota(I32, (lc, lc), 0)
    si = lax.broadcasted_iota(I32, (lc, lc), 1)
    causal = si <= ti

    def chunk_body(c, carry):
        r0 = pl.multiple_of(c * lc, lc)
        rows = pl.ds(r0, lc)
        gcol = gcol_ref[rows, :]
        grow = grow_ref[c]
        nb = n_ref[...].astype(BF16)
        for h in range(heads):
            cols = slice(h * hd, (h + 1) * hd)
            qc = q_ref[rows, cols]
            kc = k_ref[rows, cols]
            vc = v_ref[rows, cols]
            ig_c = gcol[:, h:h + 1]
            ig_r = grow[h:h + 1, :]
            lf_c = gcol[:, heads + h:heads + h + 1]
            lf_r = grow[heads + h:heads + h + 1, :]
            bcum_c = jnp.sum(jnp.where(causal, lf_r, 0.0), axis=1, keepdims=True)
            bcum_r = jnp.sum(jnp.where(ti <= si, lf_c, 0.0), axis=0, keepdims=True)
            m_st = m_ref[h:h + 1, 0:1]
            dm = jnp.where(causal, bcum_c - bcum_r + ig_r, -jnp.inf)
            inter = bcum_c + m_st
            m_t = jnp.maximum(inter, jnp.max(dm, axis=1, keepdims=True))
            dexp = jnp.exp(dm - m_t)
            sc = jnp.exp(inter - m_t)
            sco = (_dot_nt(qc, kc) * scale) * dexp
            cst = c_ref[h]
            num = _dot(sco.astype(BF16), vc) + sc * _dot(qc, cst.astype(BF16))
            qn = _dot_nt(qc, nb)[:, h:h + 1]
            den = jnp.sum(sco, axis=1, keepdims=True) + sc * qn
            hc = num * (1.0 / jnp.maximum(jnp.abs(den), jnp.exp(-m_t)))
            m_new = m_t[lc - 1:lc, :]
            b_last = bcum_c[lc - 1:lc, :]
            w_c = jnp.exp(b_last - bcum_c + ig_c - m_new) * scale
            w_r = jnp.exp(b_last - bcum_r + ig_r - m_new) * scale
            decay = jnp.exp(b_last + m_st - m_new)
            kw = kc * w_c.astype(BF16)
            c_ref[h] = decay * cst + _dot_tn(kw, vc)
            wk = _dot(jnp.broadcast_to(w_r, (SUBLANES, lc)).astype(BF16), kc)
            n_ref[h:h + 1, :] = decay * n_ref[h:h + 1, :] + wk[0:1, :]
            m_ref[h:h + 1, :] = jnp.broadcast_to(m_new, (1, LANES))
            hnrm = hc * lax.rsqrt(jnp.mean(hc * hc, axis=-1, keepdims=True) + RMS_EPS) * ng_ref[:, cols]
            zc = z_ref[rows, cols].astype(F32)
            xcc = xc_ref[rows, cols].astype(F32)
            y_ref[rows, cols] = (jax.nn.sigmoid(zc) * (hnrm + skip_ref[:, cols] * xcc)).astype(BF16)
        return carry

    lax.fori_loop(0, tm // lc, chunk_body, 0, unroll=True)
    o_ref[...] = x + _dot(y_ref[...], wout_ref[...])


def _mlstm_layer(x, g, w_in, conv_w, conv_b, w_q, w_k, w_v, w_gate, b_gate, norm_g, skip, w_out, heads):
    s, d = x.shape
    inner = w_out.array.shape[1]
    tm = min(TOKEN_TILE, s)
    ng = 2 * heads
    wqk = jnp.concatenate([_block_diag_tiles(w_q, MXU_DIM), _block_diag_tiles(w_k, MXU_DIM)],
                          axis=-1).astype(BF16)
    wv = _block_diag_tiles(w_v, MXU_DIM).astype(BF16)
    wgp = jnp.pad(w_gate, ((0, 0), (0, LANES - ng))).astype(BF16)
    wg_parts = [wgp[p * inner:(p + 1) * inner] for p in range(3)]
    bgp = jnp.pad(b_gate, (0, LANES - ng)).reshape(1, LANES)
    nqb = inner // MXU_DIM
    kern = functools.partial(_mlstm_kernel, tm=tm, inner=inner, heads=heads)
    in_specs = [_row_spec(tm, d), _const_spec((1, d)), _layer_spec(w_in),
                _const_spec((CONV_W, inner)), _const_spec((1, inner)),
                _const_spec((nqb, MXU_DIM, 2 * MXU_DIM)), _const_spec((nqb, MXU_DIM, MXU_DIM))]
    in_specs += [_const_spec((inner, LANES))] * 3
    in_specs += [_const_spec((1, LANES)), _const_spec((1, inner)), _const_spec((1, inner)),
                 _layer_spec(w_out)]
    hd = inner // heads
    scratch = [pltpu.VMEM((SUBLANES, inner), F32)]
    scratch += [pltpu.VMEM((tm, inner), BF16)] * 6
    scratch += [pltpu.VMEM((tm, LANES), F32), pltpu.VMEM((tm // ML_CHUNK, SUBLANES, ML_CHUNK), F32),
                pltpu.VMEM((heads, hd, hd), F32), pltpu.VMEM((SUBLANES, hd), F32),
                pltpu.VMEM((SUBLANES, LANES), F32)]
    return pl.pallas_call(
        kern,
        out_shape=jax.ShapeDtypeStruct((s, d), F32),
        grid=(s // tm,),
        in_specs=in_specs,
        out_specs=_row_spec(tm, d),
        scratch_shapes=scratch,
        compiler_params=_cparams(1, 56),
        name="mlstm_layer",
    )(x, g.reshape(1, d), w_in.array, conv_w, conv_b.reshape(1, inner), wqk, wv, *wg_parts, bgp,
      norm_g.reshape(1, inner), skip.reshape(1, inner), w_out.array)


def _s5_in_kernel(x_ref, g_ref, o_ref, slab_ref, *, tm):
    hn = _rms(x_ref[...], g_ref[...])
    nslab = hn.shape[1] // LANES
    for k in range(nslab):
        slab_ref[k] = hn[:, k * LANES:(k + 1) * LANES]
    for s in range(S5_CHUNK):
        for k in range(nslab):
            rows = slab_ref[k, pl.ds(s, tm // S5_CHUNK, stride=S5_CHUNK), :]
            o_ref[s, :, k * LANES:(k + 1) * LANES] = rows.astype(BF16)


def _s5_scan_kernel(xs_ref, t_ref, bc_ref, cc_ref, pre_ref, pim_ref, ys_ref,
                    sre_ref, sim_ref, cre_ref, cim_ref, *, cb, half):
    lc = S5_CHUNK

    @pl.when(pl.program_id(1) == 0)
    def _():
        cre_ref[...] = jnp.zeros_like(cre_ref)
        cim_ref[...] = jnp.zeros_like(cim_ref)

    u = jnp.concatenate([xs_ref[s] for s in range(lc)], axis=1)
    npair = (lc * LANES) // MXU_DIM
    ytiles = []
    for b in range(npair):
        acc = _dot(u[:, 0:MXU_DIM], t_ref[b])
        for a in range(1, b + 1):
            acc = acc + _dot(u[:, a * MXU_DIM:(a + 1) * MXU_DIM], t_ref[b - a])
        ytiles.append(acc)
    sinc = _dot(u, bc_ref[...])
    sre_ref[...] = sinc[:, :half]
    sim_ref[...] = sinc[:, half:]
    cin_re, cin_im = cre_ref[...], cim_ref[...]
    pr0, pi0 = pre_ref[0], pim_ref[0]
    sre_ref[0:1, :] = sre_ref[0:1, :] + (pr0 * cin_re - pi0 * cin_im)
    sim_ref[0:1, :] = sim_ref[0:1, :] + (pr0 * cin_im + pi0 * cin_re)
    xre, xim = sre_ref[...], sim_ref[...]
    row = lax.broadcasted_iota(I32, xre.shape, 0)
    d, kstep = 1, 0
    while d < cb:
        keep = row >= d
        re_sh = jnp.where(keep, pltpu.roll(xre, d, 0), 0.0)
        im_sh = jnp.where(keep, pltpu.roll(xim, d, 0), 0.0)
        pr, pi = pre_ref[kstep], pim_ref[kstep]
        xre, xim = xre + (pr * re_sh - pi * im_sh), xim + (pr * im_sh + pi * re_sh)
        d *= 2
        kstep += 1
    cre_ref[...] = xre[cb - 1:cb, :]
    cim_ref[...] = xim[cb - 1:cb, :]
    first = row == 0
    prev_re = jnp.where(first, cin_re, pltpu.roll(xre, 1, 0)).astype(BF16)
    prev_im = jnp.where(first, cin_im, pltpu.roll(xim, 1, 0)).astype(BF16)
    prev = jnp.concatenate([prev_re, prev_im], axis=1)
    for b in range(npair):
        yb = ytiles[b] + _dot(prev, cc_ref[:, b * MXU_DIM:(b + 1) * MXU_DIM])
        ys_ref[2 * b] = yb[:, :LANES]
        ys_ref[2 * b + 1] = yb[:, LANES:]


def _cmul(ar, ai, br, bi):
    return ar * br - ai * bi, ar * bi + ai * br


def _s5_operators(a_re, a_im, log_step, b_re, b_im, c_re, c_im, cb):
    lc = S5_CHUNK
    ng, np_, ni = b_re.shape
    gpv = LANES // ni
    nv = ng // gpv
    npair = lc // 2
    step = jnp.exp(log_step)[:, None]
    taus = jnp.arange(lc + 1, dtype=F32)[None, :, None]
    mag = jnp.exp((a_re * step)[:, None, :] * taus)
    ang = (a_im * step)[:, None, :] * taus
    pr, pi = mag * jnp.cos(ang), mag * jnp.sin(ang)
    lr, li = pr[:, 1], pi[:, 1]
    den = a_re * a_re + a_im * a_im
    fr, fi = _cmul(lr - 1.0, li, a_re / den, -a_im / den)
    bbr, bbi = _cmul(fr[..., None], fi[..., None], b_re, b_im)
    bbrt, bbit = bbr.transpose(0, 2, 1), bbi.transpose(0, 2, 1)
    mr, mi = _cmul(c_re[:, None], c_im[:, None], pr[:, :, None, :], pi[:, :, None, :])
    kk = jnp.sum(mr[:, :lc, None, :, :] * bbrt[:, None, :, None, :]
                 - mi[:, :lc, None, :, :] * bbit[:, None, :, None, :], axis=-1)
    grp16 = lambda r: (r // ni) % gpv
    lag_of = np.zeros((npair, 2, 2), np.int64)
    for dd in range(npair):
        for s2 in range(2):
            for t2 in range(2):
                lag_of[dd, s2, t2] = 2 * dd + t2 - s2
    sel = (np.arange(lc)[:, None, None, None] == lag_of[None]).astype(np.float32)
    sp_t = np.einsum('ldst,jk->ljdstk', sel, np.eye(ni, dtype=np.float32))
    sp_t = np.broadcast_to(sp_t[:, :, :, :, :, None, :], (lc, ni, npair, 2, 2, gpv, ni))
    sp_t = jnp.asarray(sp_t.reshape(lc * ni, npair * 2 * 2 * gpv * ni))
    kk2 = kk.transpose(0, 2, 1, 3).reshape(ng * ni, lc * ni)
    toep = _spread_masked(kk2, sp_t, grp16, grp16)
    toep = toep.reshape(nv, gpv * ni, npair * 2, 2 * gpv * ni).transpose(0, 2, 1, 3)
    toep = toep.reshape(nv, npair, MXU_DIM, MXU_DIM)
    rtaus = (lc - 1) - taus[:, :lc]
    rmag = jnp.exp((a_re * step)[:, None, :] * rtaus)
    rang = (a_im * step)[:, None, :] * rtaus
    rev_r, rev_i = rmag * jnp.cos(rang), rmag * jnp.sin(rang)
    bcr, bci = _cmul(rev_r[:, :, None, :], rev_i[:, :, None, :], bbrt[:, None], bbit[:, None])
    bcx = jnp.concatenate([bcr, bci], axis=-1).reshape(nv, gpv, lc, ni, 2 * np_).transpose(0, 2, 1, 3, 4)
    sp_b = jnp.tile(jnp.eye(2 * np_, dtype=F32).reshape(2 * np_, 2, 1, np_), (1, 1, gpv, 1)).reshape(2 * np_, -1)
    bc = _spread_masked(bcx.reshape(-1, 2 * np_), sp_b, grp16, lambda c: (c // np_) % gpv)
    bc = bc.reshape(nv, lc * gpv * ni, 2 * gpv * np_)
    ccx = jnp.stack([mr[:, 1:lc + 1], -mi[:, 1:lc + 1]], axis=1)
    ccx = ccx.reshape(nv, gpv, 2, lc, ni, np_).transpose(0, 2, 1, 5, 3, 4)
    sp_c = jnp.tile(jnp.eye(lc * ni, dtype=F32).reshape(lc * ni, lc, 1, ni), (1, 1, gpv, 1)).reshape(lc * ni, -1)
    cc = _spread_masked(ccx.reshape(-1, lc * ni), sp_c, lambda r: (r // np_) % gpv, grp16)
    cc = cc.reshape(nv, 2 * gpv * np_, lc * gpv * ni)
    nsteps = max(1, int(math.ceil(math.log2(max(cb, 2)))))
    sq = jnp.stack([pr[:, lc].reshape(1, -1), pi[:, lc].reshape(1, -1)])
    scan = []
    for _ in range(nsteps):
        scan.append(sq)
        sq = jnp.stack(_cmul(sq[0], sq[1], sq[0], sq[1]))
    scan = jnp.stack(scan, axis=1)
    return toep, bc, cc, scan[0], scan[1]


def _s5_out_kernel(x_ref, ys_ref, g_ref, d_ref, wglu_ref, bglu_ref, o_ref, slab_ref, *, tm):
    x = x_ref[...]
    nslab = x.shape[1] // LANES
    for s in range(S5_CHUNK):
        for k in range(nslab):
            slab_ref[k, pl.ds(s, tm // S5_CHUNK, stride=S5_CHUNK), :] = ys_ref[s, :, k * LANES:(k + 1) * LANES]
    yt = jnp.concatenate([slab_ref[k] for k in range(nslab)], axis=1)
    hn = _rms(x, g_ref[...])
    y = _gelu_tanh(yt + d_ref[...] * hn)
    o_ref[...] = x + y * jax.nn.sigmoid(_dot(y.astype(BF16), wglu_ref[...]) + bglu_ref[...])


def _s5_layer(x, g, a_re, a_im, log_step, b_re, b_im, c_re, c_im, d_skip, w_glu, b_glu):
    s, d = x.shape
    ngroups, pstate, gch = b_re.shape
    lc = S5_CHUNK
    nchunk = s // lc
    cb = min(S5_CHUNK_BLOCK, nchunk)
    gpv = LANES // gch
    nv = ngroups // gpv
    half = gpv * pstate
    toep, bc, cc, scan_r, scan_i = _s5_operators(a_re, a_im, log_step, b_re, b_im, c_re, c_im, cb)
    nsteps = scan_r.shape[0]
    tm = min(TOKEN_TILE, s)
    nslab = d // LANES
    step_major = pl.BlockSpec((lc, tm // lc, d), lambda i: (0, i, 0))
    xs = pl.pallas_call(
        functools.partial(_s5_in_kernel, tm=tm),
        out_shape=jax.ShapeDtypeStruct((lc, nchunk, d), BF16),
        grid=(s // tm,),
        in_specs=[_row_spec(tm, d), _const_spec((1, d))],
        out_specs=step_major,
        scratch_shapes=[pltpu.VMEM((nslab, tm, LANES), F32)],
        compiler_params=_cparams(1, 24),
        name="s5_in",
    )(x, g.reshape(1, d))
    wl = lc * LANES
    npair = lc // 2
    ys = pl.pallas_call(
        functools.partial(_s5_scan_kernel, cb=cb, half=half),
        out_shape=jax.ShapeDtypeStruct((lc, nchunk, d), F32),
        grid=(nv, nchunk // cb),
        in_specs=[pl.BlockSpec((lc, cb, LANES), lambda v, j: (0, j, v)),
                  pl.BlockSpec((None, npair, MXU_DIM, MXU_DIM), lambda v, j: (v, 0, 0, 0)),
                  pl.BlockSpec((None, wl, 2 * half), lambda v, j: (v, 0, 0)),
                  pl.BlockSpec((None, 2 * half, wl), lambda v, j: (v, 0, 0)),
                  pl.BlockSpec((nsteps, 1, half), lambda v, j: (0, 0, v)),
                  pl.BlockSpec((nsteps, 1, half), lambda v, j: (0, 0, v))],
        out_specs=pl.BlockSpec((lc, cb, LANES), lambda v, j: (0, j, v)),
        scratch_shapes=[pltpu.VMEM((cb, half), F32), pltpu.VMEM((cb, half), F32),
                        pltpu.VMEM((1, half), F32), pltpu.VMEM((1, half), F32)],
        compiler_params=_cparams(2, 52),
        name="s5_scan",
    )(xs, toep, bc, cc, scan_r, scan_i)
    return pl.pallas_call(
        functools.partial(_s5_out_kernel, tm=tm),
        out_shape=jax.ShapeDtypeStruct((s, d), F32),
        grid=(s // tm,),
        in_specs=[_row_spec(tm, d), step_major, _const_spec((1, d)), _const_spec((1, d)),
                  _layer_spec(w_glu), _const_spec((1, d))],
        out_specs=_row_spec(tm, d),
        scratch_shapes=[pltpu.VMEM((nslab, tm, LANES), F32)],
        compiler_params=_cparams(1, 32),
        name="s5_out",
    )(x, ys, g.reshape(1, d), d_skip.reshape(1, d), w_glu.array, b_glu.reshape(1, d))


def _router_kernel(x_ref, g_ref, wr_ref, br_ref, hn_ref, wts_ref, idx_ref, cnt_ref, run_ref, *, tm, ne):
    @pl.when(pl.program_id(0) == 0)
    def _():
        run_ref[...] = jnp.zeros_like(run_ref)

    hn = _rms(x_ref[...], g_ref[...])
    for k in range(hn.shape[1] // LANES):
        hn_ref[pl.ds(k, tm, stride=SUBLANES), :] = hn[:, k * LANES:(k + 1) * LANES]
    wr = wr_ref[...]
    hn_hi, wr_hi = hn.astype(BF16), wr.astype(BF16)
    hn_lo, wr_lo = (hn - hn_hi.astype(F32)).astype(BF16), (wr - wr_hi.astype(F32)).astype(BF16)
    logits = (_dot(hn_hi, wr_hi) + _dot(hn_hi, wr_lo)) + (_dot(hn_lo, wr_hi) + _dot(hn_lo, wr_lo))
    logits = logits + br_ref[...]
    lane = lax.broadcasted_iota(I32, (tm, ne), 1).astype(F32)
    m1 = jnp.max(logits, axis=1, keepdims=True)
    i1 = jnp.min(jnp.where(logits == m1, lane, float(ne)), axis=1, keepdims=True)
    sel1 = lane == i1
    rest = jnp.where(sel1, -jnp.inf, logits)
    m2 = jnp.max(rest, axis=1, keepdims=True)
    i2 = jnp.min(jnp.where(rest == m2, lane, float(ne)), axis=1, keepdims=True)
    sel2 = lane == i2
    e = jnp.exp(m2 - m1)
    w1 = 1.0 / (1.0 + e)
    w2 = e / (1.0 + e)
    sel = jnp.logical_or(sel1, sel2)
    ti = lax.broadcasted_iota(I32, (tm, tm), 0)
    si = lax.broadcasted_iota(I32, (tm, tm), 1)
    earlier = jnp.where(si < ti, 1.0, 0.0).astype(BF16)
    excl = _dot(earlier, jnp.where(sel, 1.0, 0.0).astype(BF16)) + run_ref[...]
    r1 = jnp.sum(jnp.where(sel1, excl, 0.0), axis=1, keepdims=True)
    r2 = jnp.sum(jnp.where(sel2, excl, 0.0), axis=1, keepdims=True)
    total = run_ref[...] + jnp.sum(jnp.where(sel, 1.0, 0.0), axis=0, keepdims=True)
    run_ref[...] = total
    cnt_ref[...] = total
    wts_ref[...] = jnp.where(lane == 0.0, w1, jnp.where(lane == 1.0, w2, 0.0))
    lane_w = lax.broadcasted_iota(I32, (tm, LANES), 1)
    meta = jnp.zeros((tm, LANES), F32)
    for c, val in enumerate([i1, i2, r1, r2]):
        meta = jnp.where(lane_w == c, val, meta)
    idx_ref[...] = meta.T[0:SUBLANES, :].astype(I32)


def _token_rows(t):
    return pl.ds(pl.multiple_of(t * SUBLANES, SUBLANES), SUBLANES)


def _untile_tokens(ref, n, nslab, lead=()):
    return jnp.concatenate([ref[lead + (pl.ds(k, n, stride=SUBLANES), slice(None))] for k in range(nslab)], axis=1)


def _dispatch_kernel(ends_ref, pos_ref, hn_ref, xs_ref, ring_ref, zero_ref, in_sem, out_sem, zsem, *, tg, tr, ne):
    i = pl.program_id(0)
    n = pl.num_programs(0)
    nrows = tg * SUBLANES

    def fetch(step, sl):
        src = hn_ref.at[pl.ds(pl.multiple_of(step * nrows, nrows), nrows), :]
        return pltpu.make_async_copy(src, ring_ref.at[sl], in_sem.at[sl])

    def row_copy(t, dst_row, sl):
        return pltpu.make_async_copy(ring_ref.at[sl, _token_rows(t), :], xs_ref.at[_token_rows(dst_row), :],
                                     out_sem.at[sl])

    def drain(sl):
        def body(t, carry):
            for k in range(TOP_K):
                row_copy(t, 0, sl).wait()
            return carry
        lax.fori_loop(0, tg, body, 0, unroll=8)

    @pl.when(i == 0)
    def _():
        fetch(0, 0).start()
        zero_ref[...] = jnp.zeros_like(zero_ref)
        last_tile = xs_ref.shape[0] // SUBLANES - tr
        for e in range(ne):
            for r0 in (jnp.maximum(ends_ref[e] - tr, 0), jnp.minimum(ends_ref[ne - 1] + e * tr, last_tile)):
                dst = xs_ref.at[pl.ds(pl.multiple_of(r0 * SUBLANES, tr * SUBLANES), tr * SUBLANES), :]
                cp = pltpu.make_async_copy(zero_ref, dst, zsem)
                cp.start()
                cp.wait()

    slot = i % 3
    fetch(i, slot).wait()

    @pl.when(i >= 2)
    def _():
        drain((i + 1) % 3)

    @pl.when(i + 1 < n)
    def _():
        fetch(i + 1, (i + 1) % 3).start()

    def issue(t, carry):
        for k in range(TOP_K):
            row_copy(t, pos_ref[k, t], slot).start(priority=k)
        return carry

    lax.fori_loop(0, tg, issue, 0, unroll=8)

    @pl.when(i == n - 1)
    def _():
        @pl.when(i >= 1)
        def _():
            drain((i + 2) % 3)
        drain(slot)


def _expert_kernel(te_ref, nv_ref, xs_ref, w1_ref, w3_ref, w2_ref, ys_ref, *, tr, ff_chunk):
    i = pl.program_id(0)
    nslab = w2_ref.shape[1] // LANES

    @pl.when(i < nv_ref[0])
    def _():
        xb = _untile_tokens(xs_ref, tr, nslab).astype(BF16)
        y = None
        for c in range(w1_ref.shape[1] // ff_chunk):
            sl = slice(c * ff_chunk, (c + 1) * ff_chunk)
            h = (_silu(_dot(xb, w1_ref[:, sl])) * _dot(xb, w3_ref[:, sl])).astype(BF16)
            part = _dot(h, w2_ref[sl, :])
            y = part if y is None else y + part
        for k in range(nslab):
            ys_ref[pl.ds(k, tr, stride=SUBLANES), :] = y[:, k * LANES:(k + 1) * LANES]

    @pl.when(i >= nv_ref[0])
    def _():
        ys_ref[...] = jnp.zeros_like(ys_ref)


def _combine_kernel(pos_ref, nxt_ref, x_ref, wts_ref, gf_ref, ys_ref, o_ref, buf_ref, sem, *, tg, final_norm):
    i = pl.program_id(0)
    slot = i % 2

    def row_copy(t, k, src_row, sl):
        return pltpu.make_async_copy(ys_ref.at[_token_rows(src_row), :], buf_ref.at[sl, k, _token_rows(t), :],
                                     sem.at[sl])

    def request(rows_ref, sl):
        def issue(t, carry):
            for k in range(TOP_K):
                row_copy(t, k, rows_ref[k, t], sl).start(priority=k)
            return carry
        lax.fori_loop(0, tg, issue, 0, unroll=8)

    @pl.when(i == 0)
    def _():
        request(pos_ref, 0)

    @pl.when(i + 1 < pl.num_programs(0))
    def _():
        request(nxt_ref, 1 - slot)

    def drain(t, carry):
        for k in range(TOP_K):
            row_copy(t, k, 0, slot).wait()
        return carry

    lax.fori_loop(0, tg, drain, 0, unroll=8)

    x = x_ref[...]
    nslab = x.shape[1] // LANES
    wts = wts_ref[...]
    out = x + (wts[:, 0:1] * _untile_tokens(buf_ref, tg, nslab, (slot, 0))
               + wts[:, 1:2] * _untile_tokens(buf_ref, tg, nslab, (slot, 1)))
    if final_norm:
        out = _rms(out, gf_ref[...])
    o_ref[...] = out


def _moe_layer(x, g, router, router_b, w1, w3, w2, g_final, final_norm):
    s, d = x.shape
    ne = router.shape[1]
    f = w1.array.shape[3]
    tm = min(TOKEN_TILE, s)
    tg = min(GATHER_TILE, s)
    tr = GROUP_TILE
    sub = SUBLANES
    hn, wts, idx, cnt = pl.pallas_call(
        functools.partial(_router_kernel, tm=tm, ne=ne),
        out_shape=(jax.ShapeDtypeStruct((s * sub, LANES), F32), jax.ShapeDtypeStruct((s, ne), F32),
                   jax.ShapeDtypeStruct((sub, s), I32), jax.ShapeDtypeStruct((1, ne), F32)),
        grid=(s // tm,),
        in_specs=[_row_spec(tm, d), _const_spec((1, d)), _const_spec((d, ne)), _const_spec((1, ne))],
        out_specs=(_row_spec(tm * sub, LANES), _row_spec(tm, ne), pl.BlockSpec((sub, tm), lambda i: (0, i)),
                   pl.BlockSpec((1, ne), lambda i: (0, 0))),
        scratch_shapes=[pltpu.VMEM((1, ne), F32)],
        compiler_params=_cparams(1, 24),
        name="moe_router",
    )(x, g.reshape(1, d), router, router_b.reshape(1, ne))

    counts = cnt[0].astype(I32)
    padded = ((counts + tr - 1) // tr) * tr
    ends = jnp.cumsum(padded)
    starts = ends - padded
    n_tiles = (TOP_K * s) // tr + ne
    rows = n_tiles * tr
    n_valid = ends[-1] // tr
    tile_row = jnp.minimum(jnp.arange(n_tiles, dtype=I32), n_valid - 1) * tr
    tile_expert = jnp.minimum(jnp.sum((ends[None, :] <= tile_row[:, None]).astype(I32), axis=1), ne - 1)

    expert_start = jnp.sum(jnp.where(idx[0:TOP_K, :, None] == jnp.arange(ne, dtype=I32), starts, 0), axis=-1)
    pos = expert_start + idx[TOP_K:2 * TOP_K]

    pos_spec = pl.BlockSpec((TOP_K, tg), lambda i, *_: (0, i), memory_space=pltpu.SMEM)
    any_spec = pl.BlockSpec(memory_space=pl.ANY)
    xs = pl.pallas_call(
        functools.partial(_dispatch_kernel, tg=tg, tr=tr, ne=ne),
        out_shape=jax.ShapeDtypeStruct((rows * sub, LANES), F32),
        grid_spec=pltpu.PrefetchScalarGridSpec(
            num_scalar_prefetch=1,
            grid=(s // tg,),
            in_specs=[pos_spec, any_spec],
            out_specs=any_spec,
            scratch_shapes=[pltpu.VMEM((3, tg * sub, LANES), F32), pltpu.VMEM((tr * sub, LANES), F32),
                            pltpu.SemaphoreType.DMA((3,)), pltpu.SemaphoreType.DMA((3,)),
                            pltpu.SemaphoreType.DMA],
        ),
        compiler_params=_cparams(1, 16),
        name="moe_dispatch",
    )(ends, pos, hn)

    row_tile = lambda i, te, nv: (jnp.maximum(jnp.minimum(i, nv[0] - 1), 0), 0)
    expert_w = lambda w: (lambda i, te, nv: (w.index, te[i], 0, 0))
    ys = pl.pallas_call(
        functools.partial(_expert_kernel, tr=tr, ff_chunk=MXU_DIM),
        out_shape=jax.ShapeDtypeStruct((rows * sub, LANES), F32),
        grid_spec=pltpu.PrefetchScalarGridSpec(
            num_scalar_prefetch=2,
            grid=(n_tiles,),
            in_specs=[pl.BlockSpec((tr * sub, LANES), row_tile),
                      pl.BlockSpec((None, None, d, f), expert_w(w1)),
                      pl.BlockSpec((None, None, d, f), expert_w(w3)),
                      pl.BlockSpec((None, None, f, d), expert_w(w2))],
            out_specs=pl.BlockSpec((tr * sub, LANES), lambda i, te, nv: (i, 0)),
        ),
        compiler_params=_cparams(1, 54),
        name="moe_experts",
    )(tile_expert, n_valid.reshape(1), xs, w1.array, w3.array, w2.array)

    return pl.pallas_call(
        functools.partial(_combine_kernel, tg=tg, final_norm=final_norm),
        out_shape=jax.ShapeDtypeStruct((s, d), F32),
        grid=(s // tg,),
        in_specs=[pl.BlockSpec((TOP_K, tg), lambda i: (0, i), memory_space=pltpu.SMEM),
                  pl.BlockSpec((TOP_K, tg), lambda i: (0, jnp.minimum(i + 1, s // tg - 1)),
                               memory_space=pltpu.SMEM),
                  _row_spec(tg, d), _row_spec(tg, ne), _const_spec((1, d)), any_spec],
        out_specs=_row_spec(tg, d),
        scratch_shapes=[pltpu.VMEM((2, TOP_K, tg * sub, LANES), F32), pltpu.SemaphoreType.DMA((2,))],
        compiler_params=_cparams(1, 24),
        name="moe_combine",
    )(pos, pos, x, wts, g_final.reshape(1, d), ys)


def _norm_kernel(x_ref, g_ref, o_ref):
    o_ref[...] = _rms(x_ref[...], g_ref[...])


def _final_norm(x, g):
    s, d = x.shape
    tm = min(TOKEN_TILE, s)
    return pl.pallas_call(
        _norm_kernel,
        out_shape=jax.ShapeDtypeStruct((s, d), F32),
        grid=(s // tm,),
        in_specs=[_row_spec(tm, d), _const_spec((1, d))],
        out_specs=_row_spec(tm, d),
        compiler_params=_cparams(1, 16),
        name="final_norm",
    )(x, g.reshape(1, d))


def kernel(x, norm_mix, norm_ffn, norm_final, a_w_in, a_conv_w, a_conv_b, a_w_r, a_b_r, a_w_i, a_b_i, a_lam, a_w_out, b_w_in, b_conv_w, b_conv_b, b_w_q, b_w_k, b_w_v, b_w_gate, b_b_gate, b_norm, b_skip, b_w_out, c_a_re, c_a_im, c_log_step, c_b_re, c_b_im, c_c_re, c_c_im, c_d, c_w_glu, c_b_glu, f_w1, f_w3, f_w2, e_router, e_router_b, e_w1, e_w3, e_w2):
    bsz, seq, d = x.shape
    depth = norm_mix.shape[0]
    heads = b_b_gate.shape[1] // 2
    stacked = {"a_w_in": a_w_in, "a_w_out": a_w_out, "b_w_in": b_w_in, "b_w_out": b_w_out, "c_w_glu": c_w_glu,
               "f_w1": f_w1, "f_w3": f_w3, "f_w2": f_w2, "e_w1": e_w1, "e_w3": e_w3, "e_w2": e_w2}
    rounded, whole = {}, {}
    steps = seq // min(TOKEN_TILE, seq)

    def layer_rows(name):
        return int(np.prod(stacked[name].shape[1:-1]))

    def can_ride(name, limit_bytes):
        rows = layer_rows(name)
        blk = rows // steps
        return (rows % steps == 0 and blk % (2 * SUBLANES) == 0
                and blk * stacked[name].shape[-1] * 4 <= limit_bytes)

    def bf16(name, layer):
        if (name, layer) in rounded:
            return _LayerWeight(rounded[(name, layer)], 0)
        if name not in whole:
            whole[name] = stacked[name].astype(BF16)
        return _LayerWeight(whole[name], layer)

    uses = {}
    for i in range(depth):
        for nm in {0: ["a_w_in", "a_w_out"], 1: ["b_w_in", "b_w_out"], 2: ["c_w_glu"]}[i % 3]:
            uses.setdefault((nm, i // 3), 2 * i)
        for nm in (["f_w1", "f_w3", "f_w2"] if i % 2 == 0 else ["e_w1", "e_w3", "e_w2"]):
            uses.setdefault((nm, i // 2), 2 * i + 1)

    def pending(position, expert_names):
        later = sorted((p, key) for key, p in uses.items() if p > position and key not in rounded)
        next_moe = min([p for p, (nm, _) in later if nm.startswith("e_")], default=None)
        keys = []
        for p, (nm, lyr) in later:
            if nm.startswith("e_"):
                ok = nm in expert_names and p == next_moe and can_ride(nm, 4 * 1024 * 1024)
            else:
                ok = nm not in whole and can_ride(nm, 1024 * 1024)
            if ok:
                keys.append((nm, lyr))
        jobs = [(stacked[nm].reshape(-1, stacked[nm].shape[-1]), layer_rows(nm), lyr) for nm, lyr in keys]
        return keys, jobs

    def record(keys, casts):
        for (nm, lyr), c in zip(keys, casts):
            rounded[(nm, lyr)] = c.reshape((1,) + stacked[nm].shape[1:])

    outs = []
    for bi in range(bsz):
        h = x[bi]
        for i in range(depth):
            kind, j = i % 3, i // 3
            if kind == 0:
                w_in, w_out = bf16("a_w_in", j), bf16("a_w_out", j)
                keys, jobs = pending(2 * i, ["e_w2"])
                h, casts = _rglru_layer(h, norm_mix[i], w_in, a_conv_w[j], a_conv_b[j], a_w_r[j], a_b_r[j],
                                        a_w_i[j], a_b_i[j], a_lam[j], w_out, jobs)
                record(keys, casts)
            elif kind == 1:
                h = _mlstm_layer(h, norm_mix[i], bf16("b_w_in", j), b_conv_w[j], b_conv_b[j], b_w_q[j], b_w_k[j],
                                 b_w_v[j], b_w_gate[j], b_b_gate[j], b_norm[j], b_skip[j], bf16("b_w_out", j),
                                 heads)
            else:
                h = _s5_layer(h, norm_mix[i], c_a_re[j], c_a_im[j], c_log_step[j], c_b_re[j], c_b_im[j],
                              c_c_re[j], c_c_im[j], c_d[j], bf16("c_w_glu", j), c_b_glu[j])
            fidx = i // 2
            last = i == depth - 1
            if i % 2 == 0:
                w1, w3, w2 = bf16("f_w1", fidx), bf16("f_w3", fidx), bf16("f_w2", fidx)
                keys, jobs = pending(2 * i + 1, ["e_w1", "e_w3"])
                h, casts = _ffn_dense(h, norm_ffn[i], w1, w3, w2, jobs)
                record(keys, casts)
                if last:
                    h = _final_norm(h, norm_final)
            else:
                h = _moe_layer(h, norm_ffn[i], e_router[fidx], e_router_b[fidx], bf16("e_w1", fidx),
                               bf16("e_w3", fidx), bf16("e_w2", fidx), norm_final, last)
        outs.append(h)
    return jnp.stack(outs)
```
